```python
import jax, jax.numpy as jnp
from jax import lax
import numpy as np

D_MODEL = 1024
BATCH = 8
SEQ = 2048
DEPTH = 2

MIX_WIDTH = D_MODEL
RET_WIDTH = MIX_WIDTH // 2
HG_WIDTH = MIX_WIDTH - RET_WIDTH
RET_HEADS = 4
RET_V_DIM = RET_WIDTH // RET_HEADS
RET_QK_DIM = RET_V_DIM // 2
RET_QK_WIDTH = RET_HEADS * RET_QK_DIM
HG_HEADS = 4
HG_DIM = HG_WIDTH // HG_HEADS
CHUNK = 64
D_FF = -(-8 * D_MODEL // (3 * 256)) * 256
ROPE_BASE = 10000.0
EPS = 1e-6
N_MOD = 6
IN_SPLITS = (RET_QK_WIDTH, RET_QK_WIDTH, RET_WIDTH, RET_WIDTH, HG_WIDTH, HG_WIDTH, HG_WIDTH, HG_WIDTH)
IN_WIDTH = sum(IN_SPLITS)
IN_OFFSETS = tuple(int(v) for v in np.cumsum(IN_SPLITS)[:-1])

kernel_name = "hybrid_retention_hgrn2_adaln_block"


def rmsnorm(x, w):
    xf = x.astype(jnp.float32)
    y = xf * lax.rsqrt(jnp.mean(xf * xf, axis=-1, keepdims=True) + EPS)
    return (y * w.astype(jnp.float32)).astype(x.dtype)


def head_rmsnorm(o, w):
    B, S, H, d = o.shape
    y = o * lax.rsqrt(jnp.mean(o * o, axis=-1, keepdims=True) + EPS)
    return y.reshape(B, S, H * d) * w.astype(jnp.float32)


def rotary(x, pos):
    d = x.shape[-1]
    inv = ROPE_BASE ** (-jnp.linspace(0.0, 1.0, d // 2, dtype=jnp.float32))
    theta = pos[:, None] * inv[None, :]
    cos = jnp.cos(theta)[None, :, None, :]
    sin = jnp.sin(theta)[None, :, None, :]
    x1, x2 = x[..., 0::2], x[..., 1::2]
    out = jnp.stack([x1 * cos - x2 * sin, x1 * sin + x2 * cos], axis=-1)
    return out.reshape(x.shape)


def to_chunks(x):
    B, S, H, d = x.shape
    return x.reshape(B, S // CHUNK, CHUNK, H, d).transpose(1, 0, 3, 2, 4)


def from_chunks(y):
    nC, B, H, C, d = y.shape
    return y.transpose(1, 0, 3, 2, 4).reshape(B, nC * C, H, d)


def retention_chunkwise(q, k, v):
    B, S, H, dk = q.shape
    dv = v.shape[-1]
    log_gamma = jnp.log(1.0 - jnp.power(2.0, -5.0 - jnp.arange(H, dtype=jnp.float32)))
    idx = jnp.arange(CHUNK, dtype=jnp.float32)
    rel = idx[:, None] - idx[None, :]
    inner_decay = jnp.exp(jnp.where(rel[None] >= 0, log_gamma[:, None, None] * rel[None], -jnp.inf))
    cross_decay = jnp.exp(log_gamma[:, None] * (idx[None, :] + 1.0))
    state_decay = jnp.exp(log_gamma[:, None] * (CHUNK - 1.0 - idx[None, :]))
    chunk_decay = jnp.exp(log_gamma * CHUNK)

    def step(R, inp):
        qc, kc, vc = inp
        scores = jnp.einsum('bhtd,bhsd->bhts', qc, kc) * inner_decay[None]
        o = (jnp.einsum('bhts,bhsv->bhtv', scores, vc)
             + jnp.einsum('bhtd,bhdv->bhtv', qc, R) * cross_decay[None, :, :, None])
        R = (chunk_decay[None, :, None, None] * R
             + jnp.einsum('bhsd,bhsv->bhdv', kc * state_decay[None, :, :, None], vc))
        return R, o

    R0 = jnp.zeros((B, H, dk, dv), jnp.float32)
    _, o = lax.scan(step, R0, (to_chunks(q), to_chunks(k), to_chunks(v)))
    return from_chunks(o)


def hgrn2_chunkwise(q, k, log_f, v):
    B, S, H, dk = q.shape
    dv = v.shape[-1]
    causal = jnp.tril(jnp.ones((CHUNK, CHUNK), dtype=bool))

    def step(St, inp):
        qc, kc, gc, vc = inp
        b = jnp.cumsum(gc, axis=2)
        diff = b[:, :, :, None, :] - b[:, :, None, :, :]
        decay = jnp.exp(jnp.where(causal[None, None, :, :, None], diff, -jnp.inf))
        scores = jnp.einsum('bhtsd,bhsd->bhts', decay * qc[:, :, :, None, :], kc)
        o = (jnp.einsum('bhts,bhsv->bhtv', scores, vc)
             + jnp.einsum('bhtd,bhdv->bhtv', qc * jnp.exp(b), St))
        b_last = b[:, :, -1:, :]
        St = (jnp.exp(b_last[:, :, 0, :])[..., None] * St
              + jnp.einsum('bhsd,bhsv->bhdv', kc * jnp.exp(b_last - b), vc))
        return St, o

    S0 = jnp.zeros((B, H, dk, dv), jnp.float32)
    _, o = lax.scan(step, S0, (to_chunks(q), to_chunks(k), to_chunks(log_f), to_chunks(v)))
    return from_chunks(o)


def hybrid_mixer(h, layer, w_in, ret_norm_w, hg_lower_bounds, hg_norm_w, w_out):
    B, S, _ = h.shape
    pos = jnp.arange(S, dtype=jnp.float32)
    proj = h @ w_in
    rq, rk, rv, rg, hq, hf, hi, hg = jnp.split(proj, IN_OFFSETS, axis=-1)

    rq = rotary(rq.astype(jnp.float32).reshape(B, S, RET_HEADS, RET_QK_DIM), pos)
    rk = rotary(rk.astype(jnp.float32).reshape(B, S, RET_HEADS, RET_QK_DIM), pos) * (RET_QK_DIM ** -0.5)
    rv = rv.astype(jnp.float32).reshape(B, S, RET_HEADS, RET_V_DIM)
    o_ret = retention_chunkwise(rq, rk, rv)
    o_ret = head_rmsnorm(o_ret, ret_norm_w) * jax.nn.silu(rg.astype(jnp.float32))

    hf = hf.astype(jnp.float32)
    if layer == 0:
        log_f = jax.nn.log_sigmoid(hf)
        k_in = jax.nn.sigmoid(-hf)
    else:
        probs = jax.nn.softmax(hg_lower_bounds.astype(jnp.float32), axis=0)
        lb = (jnp.cumsum(probs, axis=0) - probs[0:1])[layer]
        f = lb + (1.0 - lb) * jax.nn.sigmoid(hf)
        log_f = jnp.log(f)
        k_in = 1.0 - f
    hq = jax.nn.silu(hq.astype(jnp.float32)).reshape(B, S, HG_HEADS, HG_DIM)
    o_hg = hgrn2_chunkwise(hq,
                           k_in.reshape(B, S, HG_HEADS, HG_DIM),
                           log_f.reshape(B, S, HG_HEADS, HG_DIM),
                           hi.astype(jnp.float32).reshape(B, S, HG_HEADS, HG_DIM))
    o_hg = head_rmsnorm(o_hg, hg_norm_w) * jax.nn.silu(hg.astype(jnp.float32))

    o = jnp.concatenate([o_ret, o_hg], axis=-1).astype(h.dtype)
    return o @ w_out


def swiglu(h, w_gate, w_up, w_down):
    return (jax.nn.silu(h @ w_gate) * (h @ w_up)) @ w_down


def setup_inputs(seed: int = 0) -> dict:
    key = jax.random.key(seed)
    ks = jax.random.split(key, 16)
    f32 = jnp.float32
    nrm = lambda k, shape, s: jax.random.normal(k, shape, f32) * s
    return {
        "x": nrm(ks[0], (BATCH, SEQ, D_MODEL), 1.0),
        "c": nrm(ks[1], (BATCH, D_MODEL), 1.0),
        "w_ada": nrm(ks[2], (DEPTH, D_MODEL, N_MOD * D_MODEL), 0.5 * D_MODEL ** -0.5),
        "b_ada": nrm(ks[3], (DEPTH, N_MOD * D_MODEL), 0.01),
        "norm_mix_w": 1.0 + nrm(ks[4], (DEPTH, D_MODEL), 0.02),
        "w_in": nrm(ks[5], (DEPTH, D_MODEL, IN_WIDTH), D_MODEL ** -0.5),
        "ret_norm_w": 1.0 + nrm(ks[6], (DEPTH, RET_WIDTH), 0.02),
        "hg_lower_bounds": nrm(ks[7], (DEPTH, HG_WIDTH), 0.5),
        "hg_norm_w": 1.0 + nrm(ks[8], (DEPTH, HG_WIDTH), 0.02),
        "w_out": nrm(ks[9], (DEPTH, MIX_WIDTH, D_MODEL), MIX_WIDTH ** -0.5),
        "norm_ffn_w": 1.0 + nrm(ks[10], (DEPTH, D_MODEL), 0.02),
        "w_ffn_gate": nrm(ks[11], (DEPTH, D_MODEL, D_FF), D_MODEL ** -0.5),
        "w_ffn_up": nrm(ks[12], (DEPTH, D_MODEL, D_FF), D_MODEL ** -0.5),
        "w_ffn_down": nrm(ks[13], (DEPTH, D_FF, D_MODEL), D_FF ** -0.5),
        "final_norm_w": 1.0 + nrm(ks[14], (D_MODEL,), 0.02),
    }


def reference(x, c, w_ada, b_ada, norm_mix_w, w_in, ret_norm_w, hg_lower_bounds, hg_norm_w, w_out,
              norm_ffn_w, w_ffn_gate, w_ffn_up, w_ffn_down, final_norm_w):
    c_act = jax.nn.silu(c)
    for layer in range(DEPTH):
        mod = c_act @ w_ada[layer] + b_ada[layer]
        sh1, sc1, g1, sh2, sc2, g2 = jnp.split(mod[:, None, :], N_MOD, axis=-1)
        h = rmsnorm(x, norm_mix_w[layer]) * (1.0 + sc1) + sh1
        x = x + g1 * hybrid_mixer(h, layer, w_in[layer], ret_norm_w[layer], hg_lower_bounds,
                                  hg_norm_w[layer], w_out[layer])
        h = rmsnorm(x, norm_ffn_w[layer]) * (1.0 + sc2) + sh2
        x = x + g2 * swiglu(h, w_ffn_gate[layer], w_ffn_up[layer], w_ffn_down[layer])
    return rmsnorm(x, final_norm_w)
```

```python
import functools

import numpy as np
import jax
import jax.numpy as jnp
from jax import lax
from jax.experimental import pallas as pl
from jax.experimental.pallas import tpu as pltpu

D_MODEL = 1024
BATCH = 8
SEQ = 2048
DEPTH = 2
RET_WIDTH = 512
HG_WIDTH = 512
RET_HEADS = 4
RET_V_DIM = 128
RET_QK_DIM = 64
RET_QK_WIDTH = 256
HG_HEADS = 4
HG_DIM = 128
D_FF = 2816
ROPE_BASE = 10000.0
EPS = 1e-6
N_MOD = 6
IN_WIDTH = 3584

OFF_RQ, OFF_RK, OFF_RV, OFF_RG, OFF_HQ, OFF_HF, OFF_HI, OFF_HG = (
    0, 256, 512, 1024, 1536, 2048, 2560, 3072)
OFF_KIN = IN_WIDTH
ACT_WIDTH = IN_WIDTH + HG_WIDTH

LANES = 128
SEQ_TILE = 256
HG_CHUNK = 64
FFN_TILE = 512
FFN_CHUNK = 256
ADA_TILE = 1536
VMEM_LIMIT_BYTES = 56 * 1024 * 1024

F32 = jnp.float32
BF16 = jnp.bfloat16


def _hg_levels():
    out, m = [], HG_CHUNK // 2
    while m >= 1:
        out.append(m)
        m //= 2
    return out


HG_LEVELS = _hg_levels()
N_LEVELS = len(HG_LEVELS)
N_EXPONENTS = N_LEVELS + 2


def _hg_tables():
    c = HG_CHUNK
    r = np.arange(c)
    mats = [(r[None, :] <= r[:, None]).astype(np.float32)]
    masks = [np.eye(c, dtype=np.float32)]
    isq = [np.ones((c,), np.float32)]
    for m in HG_LEVELS:
        blk, pos = r // (2 * m), r % (2 * m)
        mid = blk * 2 * m + m
        upper = pos >= m
        u = r[None, :]
        mat = np.where(upper[:, None], (u >= mid[:, None]) & (u <= r[:, None]),
                       (u > r[:, None]) & (u < mid[:, None]))
        mats.append(mat.astype(np.float32))
        same = blk[:, None] == blk[None, :]
        masks.append((same & upper[:, None] & (~upper)[None, :]).astype(np.float32))
        isq.append(upper.astype(np.float32))
    mats.append((r[None, :] > r[:, None]).astype(np.float32))
    mstack = np.concatenate(mats, axis=0)
    mstack2 = np.concatenate([mstack, mstack], axis=1)
    isq_b = np.stack([np.broadcast_to(v[:, None], (c, LANES)) for v in isq])
    return mstack2, np.stack(masks), isq_b


def _ret_tables(chunk):
    h = np.arange(RET_HEADS, dtype=np.float32)
    log_gamma = np.log(np.float32(1.0) - np.power(np.float32(2.0), np.float32(-5.0) - h)).astype(np.float32)
    idx = np.arange(chunk, dtype=np.float32)
    rel = idx[:, None] - idx[None, :]
    inner = np.exp(np.where(rel[None] >= 0, log_gamma[:, None, None] * rel[None], -np.inf)).astype(np.float32)
    cross = np.exp(log_gamma[:, None] * (idx[None, :] + np.float32(1.0))).astype(np.float32)
    state = np.exp(log_gamma[:, None] * (np.float32(chunk) - np.float32(1.0) - idx[None, :])).astype(np.float32)
    chunk_decay = np.exp(log_gamma * np.float32(chunk)).astype(np.float32)
    cross_b = np.broadcast_to(cross[:, :, None], (RET_HEADS, chunk, RET_V_DIM)).copy()
    state_b = np.broadcast_to(state[:, :, None], (RET_HEADS, chunk, RET_QK_WIDTH)).copy()
    return inner, cross_b, state_b, [float(v) for v in chunk_decay]


def _rotary_tables():
    half = RET_QK_DIM // 2
    inv = ROPE_BASE ** (-jnp.linspace(0.0, 1.0, half, dtype=F32))
    pos = jnp.arange(SEQ, dtype=F32)
    theta = pos[:, None] * inv[None, :]
    return jnp.tile(jnp.cos(theta), (1, RET_HEADS)), jnp.tile(jnp.sin(theta), (1, RET_HEADS))


def _qk_permutation():
    half = RET_QK_DIM // 2
    perm = np.zeros((RET_QK_WIDTH,), np.int32)
    for p in range(2):
        for h in range(RET_HEADS):
            for i in range(half):
                perm[p * LANES + h * half + i] = h * RET_QK_DIM + 2 * i + p
    return perm


def _rms(x, w):
    return x * lax.rsqrt(jnp.mean(x * x, axis=-1, keepdims=True) + EPS) * w


def _silu(x):
    return x / (1.0 + jnp.exp(-x))


def _dot(a, b):
    return jnp.dot(a, b, preferred_element_type=F32)


def _dot_nt(a, b):
    return lax.dot_general(a, b, (((1,), (1,)), ((), ())), preferred_element_type=F32)


def _dot_tn(a, b):
    return lax.dot_general(a, b, (((0,), (0,)), ((), ())), preferred_element_type=F32)


def _ada_kernel(c_ref, w_ref, b_ref, o_ref):
    c_act = _silu(c_ref[...])
    o_ref[0] = _dot(c_act.astype(BF16), w_ref[0].astype(BF16)) + b_ref[0]


def _ada_call(c, w_ada, b_ada):
    n = N_MOD * D_MODEL
    return pl.pallas_call(
        _ada_kernel,
        grid=(DEPTH, n // ADA_TILE),
        in_specs=[
            pl.BlockSpec((BATCH, D_MODEL), lambda l, j: (0, 0)),
            pl.BlockSpec((1, D_MODEL, ADA_TILE), lambda l, j: (l, 0, j)),
            pl.BlockSpec((1, 1, ADA_TILE), lambda l, j: (l, 0, j)),
        ],
        out_specs=pl.BlockSpec((1, BATCH, ADA_TILE), lambda l, j: (l, 0, j)),
        out_shape=jax.ShapeDtypeStruct((DEPTH, BATCH, n), F32),
        compiler_params=pltpu.CompilerParams(
            dimension_semantics=("arbitrary", "arbitrary"), vmem_limit_bytes=VMEM_LIMIT_BYTES),
        name="adaln_mod",
    )(c, w_ada, b_ada.reshape(DEPTH, 1, n))


def _mixer_kernel(layer, chunk_decay,
                  x_ref, mod_ref, nw_ref, win_ref, cos_ref, sin_ref, inner_ref, cross_ref, sdec_ref,
                  mstack_ref, lmask_ref, isq_ref, lbs_ref, retw_ref, hgw_ref, wout_ref,
                  o_ref, act_ref, mix_ref, r_ref, st_ref):
    @pl.when(pl.program_id(1) == 0)
    def _():
        r_ref[...] = jnp.zeros_like(r_ref)
        st_ref[...] = jnp.zeros_like(st_ref)

    x = x_ref[0]
    mod = mod_ref[0]
    sh1, sc1, g1 = mod[0:1], mod[1:2], mod[2:3]
    h = (_rms(x, nw_ref[...]) * (1.0 + sc1) + sh1).astype(BF16)

    def proj(off, width):
        return _dot(h, win_ref[:, off:off + width])

    cos, sin = cos_ref[...], sin_ref[...]
    pq = proj(OFF_RQ, RET_QK_WIDTH)
    q1, q2 = pq[:, :LANES], pq[:, LANES:]
    act_ref[:, OFF_RQ:OFF_RQ + LANES] = q1 * cos - q2 * sin
    act_ref[:, OFF_RQ + LANES:OFF_RQ + 2 * LANES] = q1 * sin + q2 * cos
    pk = proj(OFF_RK, RET_QK_WIDTH)
    k1, k2 = pk[:, :LANES], pk[:, LANES:]
    k_scale = RET_QK_DIM ** -0.5
    act_ref[:, OFF_RK:OFF_RK + LANES] = (k1 * cos - k2 * sin) * k_scale
    act_ref[:, OFF_RK + LANES:OFF_RK + 2 * LANES] = (k1 * sin + k2 * cos) * k_scale
    act_ref[:, OFF_RV:OFF_RV + RET_WIDTH] = proj(OFF_RV, RET_WIDTH)
    act_ref[:, OFF_RG:OFF_RG + RET_WIDTH] = _silu(proj(OFF_RG, RET_WIDTH))
    act_ref[:, OFF_HQ:OFF_HQ + HG_WIDTH] = _silu(proj(OFF_HQ, HG_WIDTH))
    hf = proj(OFF_HF, HG_WIDTH)
    e = jnp.exp(-jnp.abs(hf))
    if layer == 0:
        log_f = jnp.minimum(hf, 0.0) - jnp.log1p(e)
        k_in = jnp.where(hf >= 0.0, e, 1.0) / (1.0 + e)
    else:
        lbs = lbs_ref[...]
        ex = jnp.exp(lbs - jnp.max(lbs, axis=0, keepdims=True))
        probs = ex / jnp.sum(ex, axis=0, keepdims=True)
        lb = jnp.sum(probs[1:layer + 1], axis=0, keepdims=True)
        sig = jnp.where(hf >= 0.0, 1.0, e) / (1.0 + e)
        f = lb + (1.0 - lb) * sig
        log_f = jnp.log(f)
        k_in = 1.0 - f
    act_ref[:, OFF_HF:OFF_HF + HG_WIDTH] = log_f
    act_ref[:, OFF_KIN:OFF_KIN + HG_WIDTH] = k_in
    act_ref[:, OFF_HI:OFF_HI + HG_WIDTH] = proj(OFF_HI, HG_WIDTH)
    act_ref[:, OFF_HG:OFF_HG + HG_WIDTH] = _silu(proj(OFF_HG, HG_WIDTH))

    q = act_ref[:, OFF_RQ:OFF_RQ + RET_QK_WIDTH].astype(BF16)
    kf = act_ref[:, OFF_RK:OFF_RK + RET_QK_WIDTH]
    lane = lax.broadcasted_iota(jnp.int32, (1, RET_QK_WIDTH), 1)
    head_of_lane = (lane % LANES) // (RET_QK_DIM // 2)
    for hh in range(RET_HEADS):
        vs = slice(hh * RET_V_DIM, (hh + 1) * RET_V_DIM)
        k_h = jnp.where(head_of_lane == hh, kf, 0.0)
        v_h = act_ref[:, OFF_RV + vs.start:OFF_RV + vs.stop].astype(BF16)
        scores = _dot_nt(q, k_h.astype(BF16)) * inner_ref[hh]
        r_h = r_ref[hh]
        o = _dot(scores.astype(BF16), v_h) + _dot(q, r_h.astype(BF16)) * cross_ref[hh]
        r_ref[hh] = chunk_decay[hh] * r_h + _dot_tn((k_h * sdec_ref[hh]).astype(BF16), v_h)
        gate = act_ref[:, OFF_RG + vs.start:OFF_RG + vs.stop]
        mix_ref[:, vs] = _rms(o, retw_ref[:, vs]) * gate

    def hg_chunk(ci, carry):
        rows = pl.ds(pl.multiple_of(ci * HG_CHUNK, HG_CHUNK), HG_CHUNK)
        g = act_ref[rows, OFF_HF:OFF_HF + HG_WIDTH]
        g_hi = g.astype(BF16)
        g_lo = (g - g_hi.astype(F32)).astype(BF16)
        expo = _dot(mstack_ref[...], jnp.concatenate([g_hi, g_lo], axis=0))
        decay = jnp.exp(expo)
        qh = act_ref[rows, OFF_HQ:OFF_HQ + HG_WIDTH]
        kh = act_ref[rows, OFF_KIN:OFF_KIN + HG_WIDTH]
        vh = act_ref[rows, OFF_HI:OFF_HI + HG_WIDTH].astype(BF16)
        gate = act_ref[rows, OFF_HG:OFF_HG + HG_WIDTH]
        d_cum = decay[0:HG_CHUNK]
        d_end = decay[(N_EXPONENTS - 1) * HG_CHUNK:]
        d_all = d_cum[HG_CHUNK - 1:HG_CHUNK]
        for hh in range(HG_HEADS):
            hs = slice(hh * HG_DIM, (hh + 1) * HG_DIM)
            qf, kk, v = qh[:, hs], kh[:, hs], vh[:, hs]
            scores = lmask_ref[0] * _dot_nt(qf.astype(BF16), kk.astype(BF16))
            for lv in range(1, N_LEVELS + 1):
                d_lv = decay[lv * HG_CHUNK:(lv + 1) * HG_CHUNK, hs]
                is_q = isq_ref[lv]
                q_t = (qf * d_lv * is_q).astype(BF16)
                k_t = (kk * d_lv * (1.0 - is_q)).astype(BF16)
                scores = scores + lmask_ref[lv] * _dot_nt(q_t, k_t)
            st = st_ref[hh]
            o = (_dot(scores.astype(BF16), v)
                 + _dot_nt((qf * d_cum[:, hs]).astype(BF16), st.astype(BF16)))
            st_ref[hh] = st * d_all[:, hs] + _dot_tn(v, (kk * d_end[:, hs]).astype(BF16))
            mix_ref[rows, RET_WIDTH + hs.start:RET_WIDTH + hs.stop] = (
                _rms(o, hgw_ref[:, hs]) * gate[:, hs])
        return carry

    lax.fori_loop(0, SEQ_TILE // HG_CHUNK, hg_chunk, 0)

    o_ref[0] = x + g1 * _dot(mix_ref[...].astype(BF16), wout_ref[...])


def _const_spec(shape):
    zeros = (0,) * len(shape)
    return pl.BlockSpec(shape, lambda b, s: zeros, pipeline_mode=pl.Buffered(1))


def _mixer_call(layer, x, mod, norm_w, w_in, cos, sin, ret_tabs, hg_tabs, lbs, ret_w, hg_w, w_out):
    inner, cross_b, state_b, chunk_decay = ret_tabs
    mstack2, lmask, isq_b = hg_tabs
    n_exp_rows = N_EXPONENTS * HG_CHUNK
    return pl.pallas_call(
        functools.partial(_mixer_kernel, layer, chunk_decay),
        grid=(BATCH, SEQ // SEQ_TILE),
        in_specs=[
            pl.BlockSpec((1, SEQ_TILE, D_MODEL), lambda b, s: (b, s, 0)),
            pl.BlockSpec((1, N_MOD, D_MODEL), lambda b, s: (b, 0, 0)),
            _const_spec((1, D_MODEL)),
            _const_spec((D_MODEL, IN_WIDTH)),
            pl.BlockSpec((SEQ_TILE, LANES), lambda b, s: (s, 0)),
            pl.BlockSpec((SEQ_TILE, LANES), lambda b, s: (s, 0)),
            _const_spec((RET_HEADS, SEQ_TILE, SEQ_TILE)),
            _const_spec((RET_HEADS, SEQ_TILE, RET_V_DIM)),
            _const_spec((RET_HEADS, SEQ_TILE, RET_QK_WIDTH)),
            _const_spec((n_exp_rows, 2 * HG_CHUNK)),
            _const_spec((N_LEVELS + 1, HG_CHUNK, HG_CHUNK)),
            _const_spec((N_LEVELS + 1, HG_CHUNK, LANES)),
            _const_spec((DEPTH, HG_WIDTH)),
            _const_spec((1, RET_WIDTH)),
            _const_spec((1, HG_WIDTH)),
            _const_spec((D_MODEL, D_MODEL)),
        ],
        out_specs=pl.BlockSpec((1, SEQ_TILE, D_MODEL), lambda b, s: (b, s, 0)),
        out_shape=jax.ShapeDtypeStruct((BATCH, SEQ, D_MODEL), F32),
        scratch_shapes=[
            pltpu.VMEM((SEQ_TILE, ACT_WIDTH), F32),
            pltpu.VMEM((SEQ_TILE, D_MODEL), F32),
            pltpu.VMEM((RET_HEADS, RET_QK_WIDTH, RET_V_DIM), F32),
            pltpu.VMEM((HG_HEADS, HG_DIM, HG_DIM), F32),
        ],
        compiler_params=pltpu.CompilerParams(
            dimension_semantics=("arbitrary", "arbitrary"), vmem_limit_bytes=VMEM_LIMIT_BYTES),
        name=f"mixer_l{layer}",
    )(x, mod, norm_w.reshape(1, D_MODEL), w_in, cos, sin,
      jnp.asarray(inner), jnp.asarray(cross_b), jnp.asarray(state_b),
      jnp.asarray(mstack2, dtype=BF16), jnp.asarray(lmask), jnp.asarray(isq_b),
      lbs, ret_w.reshape(1, RET_WIDTH), hg_w.reshape(1, HG_WIDTH), w_out)


def _ffn_kernel(final, x_ref, mod_ref, nw_ref, wg_ref, wu_ref, wd_ref, fw_ref, o_ref):
    x = x_ref[0]
    mod = mod_ref[0]
    sh2, sc2, g2 = mod[3:4], mod[4:5], mod[5:6]
    h = (_rms(x, nw_ref[...]) * (1.0 + sc2) + sh2).astype(BF16)
    acc = jnp.zeros((FFN_TILE, D_MODEL), F32)
    for j in range(D_FF // FFN_CHUNK):
        cs = slice(j * FFN_CHUNK, (j + 1) * FFN_CHUNK)
        a = _silu(_dot(h, wg_ref[:, cs])) * _dot(h, wu_ref[:, cs])
        acc = acc + _dot(a.astype(BF16), wd_ref[cs, :])
    y = x + g2 * acc
    if final:
        y = _rms(y, fw_ref[...])
    o_ref[0] = y


def _ffn_call(final, x, mod, norm_w, w_gate, w_up, w_down, final_w):
    return pl.pallas_call(
        functools.partial(_ffn_kernel, final),
        grid=(BATCH, SEQ // FFN_TILE),
        in_specs=[
            pl.BlockSpec((1, FFN_TILE, D_MODEL), lambda b, s: (b, s, 0)),
            pl.BlockSpec((1, N_MOD, D_MODEL), lambda b, s: (b, 0, 0)),
            _const_spec((1, D_MODEL)),
            _const_spec((D_MODEL, D_FF)),
            _const_spec((D_MODEL, D_FF)),
            _const_spec((D_FF, D_MODEL)),
            _const_spec((1, D_MODEL)),
        ],
        out_specs=pl.BlockSpec((1, FFN_TILE, D_MODEL), lambda b, s: (b, s, 0)),
        out_shape=jax.ShapeDtypeStruct((BATCH, SEQ, D_MODEL), F32),
        compiler_params=pltpu.CompilerParams(
            dimension_semantics=("arbitrary", "arbitrary"), vmem_limit_bytes=VMEM_LIMIT_BYTES),
        name="swiglu_final" if final else "swiglu",
    )(x, mod, norm_w.reshape(1, D_MODEL), w_gate, w_up, w_down, final_w.reshape(1, D_MODEL))


@jax.jit
def kernel(x, c, w_ada, b_ada, norm_mix_w, w_in, ret_norm_w, hg_lower_bounds, hg_norm_w, w_out,
           norm_ffn_w, w_ffn_gate, w_ffn_up, w_ffn_down, final_norm_w):
    assert x.shape == (BATCH, SEQ, D_MODEL) and x.dtype == F32
    mod_all = _ada_call(c, w_ada, b_ada).reshape(DEPTH, BATCH, N_MOD, D_MODEL)
    cos, sin = _rotary_tables()
    ret_tabs = _ret_tables(SEQ_TILE)
    hg_tabs = _hg_tables()
    perm = _qk_permutation()
    col_order = np.concatenate([perm, RET_QK_WIDTH + perm, np.arange(2 * RET_QK_WIDTH, IN_WIDTH)])
    for layer in range(DEPTH):
        w_in_l = w_in[layer][:, col_order].astype(BF16)
        x = _mixer_call(layer, x, mod_all[layer], norm_mix_w[layer], w_in_l, cos, sin, ret_tabs, hg_tabs,
                        hg_lower_bounds, ret_norm_w[layer], hg_norm_w[layer], w_out[layer].astype(BF16))
        x = _ffn_call(layer == DEPTH - 1, x, mod_all[layer], norm_ffn_w[layer],
                      w_ffn_gate[layer].astype(BF16), w_ffn_up[layer].astype(BF16),
                      w_ffn_down[layer].astype(BF16), final_norm_w)
    return x
```

```python
import functools

import numpy as np
import jax
import jax.numpy as jnp
from jax import lax
from jax.experimental import pallas as pl
from jax.experimental.pallas import tpu as pltpu

D_MODEL = 1024
BATCH = 8
SEQ = 2048
DEPTH = 2
RET_WIDTH = 512
HG_WIDTH = 512
RET_HEADS = 4
RET_V_DIM = 128
RET_QK_DIM = 64
RET_QK_WIDTH = 256
HG_HEADS = 4
HG_DIM = 128
D_FF = 2816
ROPE_BASE = 10000.0
EPS = 1e-6
N_MOD = 6
IN_WIDTH = 3584

OFF_RQ, OFF_RK, OFF_RV, OFF_RG, OFF_HQ, OFF_HF, OFF_HI, OFF_HG = (
    0, 256, 512, 1024, 1536, 2048, 2560, 3072)
OFF_KIN = IN_WIDTH
ACT_WIDTH = IN_WIDTH + HG_WIDTH

LANES = 128
SEQ_TILE = 256
HG_CHUNK = 64
FFN_TILE = 512
FFN_CHUNK = 256
ADA_TILE = 1536
VMEM_LIMIT_BYTES = 56 * 1024 * 1024

F32 = jnp.float32
BF16 = jnp.bfloat16


def _hg_levels():
    out, m = [], HG_CHUNK // 2
    while m >= 1:
        out.append(m)
        m //= 2
    return out


HG_LEVELS = _hg_levels()
N_LEVELS = len(HG_LEVELS)
N_EXPONENTS = N_LEVELS + 2


def _hg_tables():
    c = HG_CHUNK
    r = np.arange(c)
    mats = [(r[None, :] <= r[:, None]).astype(np.float32)]
    masks = [np.eye(c, dtype=np.float32)]
    for m in HG_LEVELS:
        blk, pos = r // (2 * m), r % (2 * m)
        mid = blk * 2 * m + m
        upper = pos >= m
        u = r[None, :]
        mat = np.where(upper[:, None], (u >= mid[:, None]) & (u <= r[:, None]),
                       (u > r[:, None]) & (u < mid[:, None]))
        mats.append(mat.astype(np.float32))
        same = blk[:, None] == blk[None, :]
        masks.append((same & upper[:, None] & (~upper)[None, :]).astype(np.float32))
    mats.append((r[None, :] > r[:, None]).astype(np.float32))
    mstack = np.concatenate(mats, axis=0)
    mstack2 = np.concatenate([mstack, mstack], axis=1)
    return mstack2, np.stack(masks)


def _ret_tables(chunk):
    h = np.arange(RET_HEADS, dtype=np.float32)
    log_gamma = np.log(np.float32(1.0) - np.power(np.float32(2.0), np.float32(-5.0) - h)).astype(np.float32)
    idx = np.arange(chunk, dtype=np.float32)
    rel = idx[:, None] - idx[None, :]
    inner = np.exp(np.where(rel[None] >= 0, log_gamma[:, None, None] * rel[None], -np.inf)).astype(np.float32)
    cross = np.exp(log_gamma[:, None] * (idx[None, :] + np.float32(1.0))).astype(np.float32)
    state = np.exp(log_gamma[:, None] * (np.float32(chunk) - np.float32(1.0) - idx[None, :])).astype(np.float32)
    chunk_decay = np.exp(log_gamma * np.float32(chunk)).astype(np.float32)
    cross_b = np.broadcast_to(cross[:, :, None], (RET_HEADS, chunk, RET_V_DIM)).copy()
    state_b = np.broadcast_to(state[:, :, None], (RET_HEADS, chunk, RET_QK_WIDTH)).copy()
    return inner, cross_b, state_b, [float(v) for v in chunk_decay]


def _rotary_tables():
    half = RET_QK_DIM // 2
    inv = ROPE_BASE ** (-jnp.linspace(0.0, 1.0, half, dtype=F32))
    pos = jnp.arange(SEQ, dtype=F32)
    theta = pos[:, None] * inv[None, :]
    return jnp.tile(jnp.cos(theta), (1, RET_HEADS)), jnp.tile(jnp.sin(theta), (1, RET_HEADS))


def _qk_permutation():
    half = RET_QK_DIM // 2
    perm = np.zeros((RET_QK_WIDTH,), np.int32)
    for p in range(2):
        for h in range(RET_HEADS):
            for i in range(half):
                perm[p * LANES + h * half + i] = h * RET_QK_DIM + 2 * i + p
    return perm


def _rms(x, w):
    return x * lax.rsqrt(jnp.mean(x * x, axis=-1, keepdims=True) + EPS) * w


def _silu(x):
    return x / (1.0 + jnp.exp(-x))


def _dot(a, b):
    return jnp.dot(a, b, preferred_element_type=F32)


def _dot_nt(a, b):
    return lax.dot_general(a, b, (((1,), (1,)), ((), ())), preferred_element_type=F32)


def _dot_tn(a, b):
    return lax.dot_general(a, b, (((0,), (0,)), ((), ())), preferred_element_type=F32)


def _ada_kernel(c_ref, w_ref, b_ref, o_ref):
    c_act = _silu(c_ref[...])
    o_ref[0] = _dot(c_act.astype(BF16), w_ref[0].astype(BF16)) + b_ref[0]


def _ada_call(c, w_ada, b_ada):
    n = N_MOD * D_MODEL
    return pl.pallas_call(
        _ada_kernel,
        grid=(DEPTH, n // ADA_TILE),
        in_specs=[
            pl.BlockSpec((BATCH, D_MODEL), lambda l, j: (0, 0)),
            pl.BlockSpec((1, D_MODEL, ADA_TILE), lambda l, j: (l, 0, j)),
            pl.BlockSpec((1, 1, ADA_TILE), lambda l, j: (l, 0, j)),
        ],
        out_specs=pl.BlockSpec((1, BATCH, ADA_TILE), lambda l, j: (l, 0, j)),
        out_shape=jax.ShapeDtypeStruct((DEPTH, BATCH, n), F32),
        compiler_params=pltpu.CompilerParams(
            dimension_semantics=("arbitrary", "arbitrary"), vmem_limit_bytes=VMEM_LIMIT_BYTES),
        name="adaln_mod",
    )(c, w_ada, b_ada.reshape(DEPTH, 1, n))


def _mixer_kernel(layer, chunk_decay,
                  x_ref, mod_ref, nw_ref, wqk_ref, wrest_ref, cos_ref, sin_ref, inner_ref, cross_ref, sdec_ref,
                  mstack_ref, lmask_ref, lbs_ref, retw_ref, hgw_ref, wout_ref,
                  o_ref, act_ref, dec_ref, mix_ref, r_ref, st_ref):
    @pl.when(pl.program_id(1) == 0)
    def _():
        r_ref[...] = jnp.zeros_like(r_ref)
        st_ref[...] = jnp.zeros_like(st_ref)

    x = x_ref[0]
    mod = mod_ref[0]
    sh1, sc1, g1 = mod[0:1], mod[1:2], mod[2:3]
    h = (_rms(x, nw_ref[...]) * (1.0 + sc1) + sh1).astype(BF16)

    def proj(off, width):
        if off < OFF_RV:
            return _dot(h, wqk_ref[:, off:off + width])
        return _dot(h, wrest_ref[:, off - OFF_RV:off - OFF_RV + width])

    n_chunks = SEQ_TILE // HG_CHUNK
    cos, sin = cos_ref[...], sin_ref[...]
    k_scale = RET_QK_DIM ** -0.5

    hf = proj(OFF_HF, HG_WIDTH)
    e = jnp.exp(-jnp.abs(hf))
    if layer == 0:
        log_f = jnp.minimum(hf, 0.0) - jnp.log1p(e)
        k_in = jnp.where(hf >= 0.0, e, 1.0) / (1.0 + e)
    else:
        lbs = lbs_ref[...]
        ex = jnp.exp(lbs - jnp.max(lbs, axis=0, keepdims=True))
        probs = ex / jnp.sum(ex, axis=0, keepdims=True)
        lb = jnp.sum(probs[1:layer + 1], axis=0, keepdims=True)
        sig = jnp.where(hf >= 0.0, 1.0, e) / (1.0 + e)
        f = lb + (1.0 - lb) * sig
        log_f = jnp.log(f)
        k_in = 1.0 - f
    act_ref[:, OFF_HF:OFF_HF + HG_WIDTH] = log_f
    act_ref[:, OFF_KIN:OFF_KIN + HG_WIDTH] = k_in
    act_ref[:, OFF_HQ:OFF_HQ + HG_WIDTH] = _silu(proj(OFF_HQ, HG_WIDTH))
    act_ref[:, OFF_HI:OFF_HI + HG_WIDTH] = proj(OFF_HI, HG_WIDTH)

    for ci in range(n_chunks):
        g = act_ref[ci * HG_CHUNK:(ci + 1) * HG_CHUNK, OFF_HF:OFF_HF + HG_WIDTH]
        g_hi = g.astype(BF16)
        g_lo = (g - g_hi.astype(F32)).astype(BF16)
        expo = _dot(mstack_ref[...], jnp.concatenate([g_hi, g_lo], axis=0))
        dec_ref[ci] = jnp.exp(expo)

    pq = proj(OFF_RQ, RET_QK_WIDTH)
    q1, q2 = pq[:, :LANES], pq[:, LANES:]
    act_ref[:, OFF_RQ:OFF_RQ + LANES] = q1 * cos - q2 * sin
    act_ref[:, OFF_RQ + LANES:OFF_RQ + 2 * LANES] = q1 * sin + q2 * cos
    pk = proj(OFF_RK, RET_QK_WIDTH)
    k1, k2 = pk[:, :LANES], pk[:, LANES:]
    act_ref[:, OFF_RK:OFF_RK + LANES] = (k1 * cos - k2 * sin) * k_scale
    act_ref[:, OFF_RK + LANES:OFF_RK + 2 * LANES] = (k1 * sin + k2 * cos) * k_scale
    act_ref[:, OFF_RV:OFF_RV + RET_WIDTH] = proj(OFF_RV, RET_WIDTH)

    pair_mask = [lmask_ref[lv] > 0.5 for lv in range(N_LEVELS + 1)]
    hg_scores, hg_upd = {}, {}
    for ci in range(n_chunks):
        rows = slice(ci * HG_CHUNK, (ci + 1) * HG_CHUNK)
        for hh in range(HG_HEADS):
            hs = slice(hh * HG_DIM, (hh + 1) * HG_DIM)
            qf = act_ref[rows, OFF_HQ + hs.start:OFF_HQ + hs.stop]
            kk = act_ref[rows, OFF_KIN + hs.start:OFF_KIN + hs.stop]
            scores = jnp.where(pair_mask[0], _dot_nt(qf.astype(BF16), kk.astype(BF16)), 0.0)
            for lv in range(1, N_LEVELS + 1):
                d_lv = dec_ref[ci, lv * HG_CHUNK:(lv + 1) * HG_CHUNK, hs]
                pair = _dot_nt((qf * d_lv).astype(BF16), (kk * d_lv).astype(BF16))
                scores = jnp.where(pair_mask[lv], pair, scores)
            hg_scores[ci, hh] = scores.astype(BF16)
    for ci in range(n_chunks):
        rows = slice(ci * HG_CHUNK, (ci + 1) * HG_CHUNK)
        for hh in range(HG_HEADS):
            hs = slice(hh * HG_DIM, (hh + 1) * HG_DIM)
            kk = act_ref[rows, OFF_KIN + hs.start:OFF_KIN + hs.stop]
            v = act_ref[rows, OFF_HI + hs.start:OFF_HI + hs.stop].astype(BF16)
            d_end = dec_ref[ci, (N_EXPONENTS - 1) * HG_CHUNK:, hs]
            hg_upd[ci, hh] = _dot_tn(v, (kk * d_end).astype(BF16))

    q = act_ref[:, OFF_RQ:OFF_RQ + RET_QK_WIDTH].astype(BF16)
    kf = act_ref[:, OFF_RK:OFF_RK + RET_QK_WIDTH]
    lane = lax.broadcasted_iota(jnp.int32, (1, RET_QK_WIDTH), 1)
    head_of_lane = (lane % LANES) // (RET_QK_DIM // 2)
    ret_scores, ret_cross = [], []
    for hh in range(RET_HEADS):
        vs = slice(hh * RET_V_DIM, (hh + 1) * RET_V_DIM)
        k_h = jnp.where(head_of_lane == hh, kf, 0.0)
        v_h = act_ref[:, OFF_RV + vs.start:OFF_RV + vs.stop].astype(BF16)
        ret_scores.append((_dot_nt(q, k_h.astype(BF16)) * inner_ref[hh]).astype(BF16))
        r_h = r_ref[hh]
        ret_cross.append(_dot(q, r_h.astype(BF16)) * cross_ref[hh])
        r_ref[hh] = chunk_decay[hh] * r_h + _dot_tn((k_h * sdec_ref[hh]).astype(BF16), v_h)

    act_ref[:, OFF_RG:OFF_RG + RET_WIDTH] = _silu(proj(OFF_RG, RET_WIDTH))
    act_ref[:, OFF_HG:OFF_HG + HG_WIDTH] = _silu(proj(OFF_HG, HG_WIDTH))

    for hh in range(HG_HEADS):
        hs = slice(hh * HG_DIM, (hh + 1) * HG_DIM)
        st = st_ref[hh]
        for ci in range(n_chunks):
            rows = slice(ci * HG_CHUNK, (ci + 1) * HG_CHUNK)
            qf = act_ref[rows, OFF_HQ + hs.start:OFF_HQ + hs.stop]
            v = act_ref[rows, OFF_HI + hs.start:OFF_HI + hs.stop].astype(BF16)
            d_cum = dec_ref[ci, 0:HG_CHUNK, hs]
            o = (_dot(hg_scores[ci, hh], v)
                 + _dot_nt((qf * d_cum).astype(BF16), st.astype(BF16)))
            st = st * d_cum[HG_CHUNK - 1:HG_CHUNK] + hg_upd[ci, hh]
            gate = act_ref[rows, OFF_HG + hs.start:OFF_HG + hs.stop]
            mix_ref[rows, RET_WIDTH + hs.start:RET_WIDTH + hs.stop] = _rms(o, hgw_ref[:, hs]) * gate
        st_ref[hh] = st
    for hh in range(RET_HEADS):
        vs = slice(hh * RET_V_DIM, (hh + 1) * RET_V_DIM)
        v_h = act_ref[:, OFF_RV + vs.start:OFF_RV + vs.stop].astype(BF16)
        o = _dot(ret_scores[hh], v_h) + ret_cross[hh]
        gate = act_ref[:, OFF_RG + vs.start:OFF_RG + vs.stop]
        mix_ref[:, vs] = _rms(o, retw_ref[:, vs]) * gate

    mixed = (_dot(mix_ref[:, RET_WIDTH:].astype(BF16), wout_ref[RET_WIDTH:, :])
             + _dot(mix_ref[:, :RET_WIDTH].astype(BF16), wout_ref[:RET_WIDTH, :]))
    o_ref[0] = x + g1 * mixed


def _const_spec(shape):
    zeros = (0,) * len(shape)
    return pl.BlockSpec(shape, lambda b, s: zeros, pipeline_mode=pl.Buffered(1))


def _layer_spec(layer, shape):
    idx = (layer,) + (0,) * len(shape)
    return pl.BlockSpec((None,) + tuple(shape), lambda b, s: idx, pipeline_mode=pl.Buffered(1))


def _mixer_call(layer, x, mod, norm_w, w_qk, w_rest, cos, sin, ret_tabs, hg_tabs, lbs, ret_w, hg_w, w_out):
    inner, cross_b, state_b, chunk_decay = ret_tabs
    mstack2, lmask = hg_tabs
    n_exp_rows = N_EXPONENTS * HG_CHUNK
    return pl.pallas_call(
        functools.partial(_mixer_kernel, layer, chunk_decay),
        grid=(BATCH, SEQ // SEQ_TILE),
        in_specs=[
            pl.BlockSpec((1, SEQ_TILE, D_MODEL), lambda b, s: (b, s, 0)),
            pl.BlockSpec((None, 1, N_MOD, D_MODEL), lambda b, s: (layer, b, 0, 0)),
            _layer_spec(layer, (1, D_MODEL)),
            _layer_spec(layer, (D_MODEL, 2 * RET_QK_WIDTH)),
            _layer_spec(layer, (D_MODEL, IN_WIDTH - 2 * RET_QK_WIDTH)),
            pl.BlockSpec((SEQ_TILE, LANES), lambda b, s: (s, 0)),
            pl.BlockSpec((SEQ_TILE, LANES), lambda b, s: (s, 0)),
            _const_spec((RET_HEADS, SEQ_TILE, SEQ_TILE)),
            _const_spec((RET_HEADS, SEQ_TILE, RET_V_DIM)),
            _const_spec((RET_HEADS, SEQ_TILE, RET_QK_WIDTH)),
            _const_spec((n_exp_rows, 2 * HG_CHUNK)),
            _const_spec((N_LEVELS + 1, HG_CHUNK, HG_CHUNK)),
            _const_spec((DEPTH, HG_WIDTH)),
            _layer_spec(layer, (1, RET_WIDTH)),
            _layer_spec(layer, (1, HG_WIDTH)),
            _layer_spec(layer, (D_MODEL, D_MODEL)),
        ],
        out_specs=pl.BlockSpec((1, SEQ_TILE, D_MODEL), lambda b, s: (b, s, 0)),
        out_shape=jax.ShapeDtypeStruct((BATCH, SEQ, D_MODEL), F32),
        scratch_shapes=[
            pltpu.VMEM((SEQ_TILE, ACT_WIDTH), F32),
            pltpu.VMEM((SEQ_TILE // HG_CHUNK, N_EXPONENTS * HG_CHUNK, HG_WIDTH), F32),
            pltpu.VMEM((SEQ_TILE, D_MODEL), F32),
            pltpu.VMEM((RET_HEADS, RET_QK_WIDTH, RET_V_DIM), F32),
            pltpu.VMEM((HG_HEADS, HG_DIM, HG_DIM), F32),
        ],
        compiler_params=pltpu.CompilerParams(
            dimension_semantics=("arbitrary", "arbitrary"), vmem_limit_bytes=VMEM_LIMIT_BYTES),
        name=f"mixer_l{layer}",
    )(x, mod, norm_w, w_qk, w_rest, cos, sin,
      jnp.asarray(inner), jnp.asarray(cross_b), jnp.asarray(state_b),
      jnp.asarray(mstack2, dtype=BF16), jnp.asarray(lmask),
      lbs, ret_w, hg_w, w_out)


def _ffn_kernel(final, x_ref, mod_ref, nw_ref, wg_ref, wu_ref, wd_ref, fw_ref, o_ref):
    x = x_ref[0]
    mod = mod_ref[0]
    sh2, sc2, g2 = mod[3:4], mod[4:5], mod[5:6]
    h = (_rms(x, nw_ref[...]) * (1.0 + sc2) + sh2).astype(BF16)
    acc = jnp.zeros((FFN_TILE, D_MODEL), F32)
    for j in range(D_FF // FFN_CHUNK):
        cs = slice(j * FFN_CHUNK, (j + 1) * FFN_CHUNK)
        a = _silu(_dot(h, wg_ref[:, cs])) * _dot(h, wu_ref[:, cs])
        acc = acc + _dot(a.astype(BF16), wd_ref[cs, :])
    y = x + g2 * acc
    if final:
        y = _rms(y, fw_ref[...])
    o_ref[0] = y


def _ffn_call(layer, x, mod, norm_w, w_gate, w_up, w_down, final_w):
    final = layer == DEPTH - 1
    return pl.pallas_call(
        functools.partial(_ffn_kernel, final),
        grid=(BATCH, SEQ // FFN_TILE),
        in_specs=[
            pl.BlockSpec((1, FFN_TILE, D_MODEL), lambda b, s: (b, s, 0)),
            pl.BlockSpec((None, 1, N_MOD, D_MODEL), lambda b, s: (layer, b, 0, 0)),
            _layer_spec(layer, (1, D_MODEL)),
            _layer_spec(layer, (D_MODEL, D_FF)),
            _layer_spec(layer, (D_MODEL, D_FF)),
            _layer_spec(layer, (D_FF, D_MODEL)),
            _const_spec((1, D_MODEL)),
        ],
        out_specs=pl.BlockSpec((1, FFN_TILE, D_MODEL), lambda b, s: (b, s, 0)),
        out_shape=jax.ShapeDtypeStruct((BATCH, SEQ, D_MODEL), F32),
        compiler_params=pltpu.CompilerParams(
            dimension_semantics=("arbitrary", "arbitrary"), vmem_limit_bytes=VMEM_LIMIT_BYTES),
        name="swiglu_final" if final else "swiglu",
    )(x, mod, norm_w, w_gate, w_up, w_down, final_w.reshape(1, D_MODEL))


@jax.jit
def kernel(x, c, w_ada, b_ada, norm_mix_w, w_in, ret_norm_w, hg_lower_bounds, hg_norm_w, w_out,
           norm_ffn_w, w_ffn_gate, w_ffn_up, w_ffn_down, final_norm_w):
    assert x.shape == (BATCH, SEQ, D_MODEL) and x.dtype == F32
    mod_all = _ada_call(c, w_ada, b_ada).reshape(DEPTH, BATCH, N_MOD, D_MODEL)
    cos, sin = _rotary_tables()
    ret_tabs = _ret_tables(SEQ_TILE)
    hg_tabs = _hg_tables()
    perm = _qk_permutation()
    qk_order = np.concatenate([perm, RET_QK_WIDTH + perm])
    w_qk = w_in[:, :, :2 * RET_QK_WIDTH][:, :, qk_order].astype(BF16)
    w_rest = w_in[:, :, 2 * RET_QK_WIDTH:].astype(BF16)
    w_out_b = w_out.astype(BF16)
    w_gate_b, w_up_b, w_down_b = (w.astype(BF16) for w in (w_ffn_gate, w_ffn_up, w_ffn_down))
    norm_mix = norm_mix_w.reshape(DEPTH, 1, D_MODEL)
    norm_ffn = norm_ffn_w.reshape(DEPTH, 1, D_MODEL)
    ret_w = ret_norm_w.reshape(DEPTH, 1, RET_WIDTH)
    hg_w = hg_norm_w.reshape(DEPTH, 1, HG_WIDTH)
    for layer in range(DEPTH):
        x = _mixer_call(layer, x, mod_all, norm_mix, w_qk, w_rest, cos, sin, ret_tabs, hg_tabs,
                        hg_lower_bounds, ret_w, hg_w, w_out_b)
        x = _ffn_call(layer, x, mod_all, norm_ffn, w_gate_b, w_up_b, w_down_b, final_norm_w)
    return x
```

```python
import functools

import numpy as np
import jax
import jax.numpy as jnp
from jax import lax
from jax.experimental import pallas as pl
from jax.experimental.pallas import tpu as pltpu

D_MODEL = 1024
BATCH = 8
SEQ = 2048
DEPTH = 2
RET_WIDTH = 512
HG_WIDTH = 512
RET_HEADS = 4
RET_V_DIM = 128
RET_QK_DIM = 64
RET_QK_WIDTH = 256
HG_HEADS = 4
HG_DIM = 128
D_FF = 2816
ROPE_BASE = 10000.0
EPS = 1e-6
N_MOD = 6
IN_WIDTH = 3584

OFF_RQ, OFF_RK, OFF_RV, OFF_RG, OFF_HQ, OFF_HF, OFF_HI, OFF_HG = (
    0, 256, 512, 1024, 1536, 2048, 2560, 3072)
OFF_KIN = IN_WIDTH
ACT_WIDTH = IN_WIDTH + HG_WIDTH

LANES = 128
SEQ_TILE = 512
RET_CHUNK = 256
HG_CHUNK = 64
FFN_TILE = 1024
FFN_ROWS = 512
FFN_CHUNK = 256
ADA_TILE = 1536
VMEM_LIMIT_BYTES = 56 * 1024 * 1024

F32 = jnp.float32
BF16 = jnp.bfloat16


def _hg_levels():
    out, m = [], HG_CHUNK // 2
    while m >= 1:
        out.append(m)
        m //= 2
    return out


HG_LEVELS = _hg_levels()
N_LEVELS = len(HG_LEVELS)
N_EXPONENTS = N_LEVELS + 2


def _hg_tables():
    c = HG_CHUNK
    r = np.arange(c)
    mats = [(r[None, :] <= r[:, None]).astype(np.float32)]
    masks = [np.eye(c, dtype=np.float32)]
    for m in HG_LEVELS:
        blk, pos = r // (2 * m), r % (2 * m)
        mid = blk * 2 * m + m
        upper = pos >= m
        u = r[None, :]
        mat = np.where(upper[:, None], (u >= mid[:, None]) & (u <= r[:, None]),
                       (u > r[:, None]) & (u < mid[:, None]))
        mats.append(mat.astype(np.float32))
        same = blk[:, None] == blk[None, :]
        masks.append((same & upper[:, None] & (~upper)[None, :]).astype(np.float32))
    mats.append((r[None, :] > r[:, None]).astype(np.float32))
    mstack = np.concatenate(mats, axis=0)
    mstack2 = np.concatenate([mstack, mstack], axis=1)
    return mstack2, np.stack(masks)


def _ret_tables(chunk):
    h = np.arange(RET_HEADS, dtype=np.float32)
    log_gamma = np.log(np.float32(1.0) - np.power(np.float32(2.0), np.float32(-5.0) - h)).astype(np.float32)
    idx = np.arange(chunk, dtype=np.float32)
    rel = idx[:, None] - idx[None, :]
    inner = np.exp(np.where(rel[None] >= 0, log_gamma[:, None, None] * rel[None], -np.inf)).astype(np.float32)
    cross = np.exp(log_gamma[:, None] * (idx[None, :] + np.float32(1.0))).astype(np.float32)
    state = np.exp(log_gamma[:, None] * (np.float32(chunk) - np.float32(1.0) - idx[None, :])).astype(np.float32)
    chunk_decay = np.exp(log_gamma * np.float32(chunk)).astype(np.float32)
    cross_b = np.broadcast_to(cross[:, :, None], (RET_HEADS, chunk, RET_V_DIM)).copy()
    state_b = np.broadcast_to(state[:, :, None], (RET_HEADS, chunk, RET_QK_WIDTH)).copy()
    return inner, cross_b, state_b, [float(v) for v in chunk_decay]


def _rotary_tables():
    half = RET_QK_DIM // 2
    inv = ROPE_BASE ** (-jnp.linspace(0.0, 1.0, half, dtype=F32))
    pos = jnp.arange(SEQ, dtype=F32)
    theta = pos[:, None] * inv[None, :]
    return jnp.tile(jnp.cos(theta), (1, RET_HEADS)), jnp.tile(jnp.sin(theta), (1, RET_HEADS))


def _deinterleave_qk(w_qk):
    half = RET_QK_DIM // 2
    lead = w_qk.shape[:-1]
    w = w_qk.reshape(lead + (2, RET_HEADS, half, 2))
    w = jnp.moveaxis(w, -1, -3)
    return w.reshape(lead + (2 * RET_QK_WIDTH,))


def _rms(x, w):
    return x * lax.rsqrt(jnp.mean(x * x, axis=-1, keepdims=True) + EPS) * w


def _silu(x):
    return x / (1.0 + jnp.exp(-x))


def _dot(a, b):
    return jnp.dot(a, b, preferred_element_type=F32)


def _dot_nt(a, b):
    return lax.dot_general(a, b, (((1,), (1,)), ((), ())), preferred_element_type=F32)


def _dot_tn(a, b):
    return lax.dot_general(a, b, (((0,), (0,)), ((), ())), preferred_element_type=F32)


def _ada_kernel(c_ref, w_ref, b_ref, o_ref):
    c_act = _silu(c_ref[...])
    o_ref[0] = _dot(c_act.astype(BF16), w_ref[0].astype(BF16)) + b_ref[0]


def _ada_call(c, w_ada, b_ada):
    n = N_MOD * D_MODEL
    return pl.pallas_call(
        _ada_kernel,
        grid=(DEPTH, n // ADA_TILE),
        in_specs=[
            pl.BlockSpec((BATCH, D_MODEL), lambda l, j: (0, 0)),
            pl.BlockSpec((1, D_MODEL, ADA_TILE), lambda l, j: (l, 0, j)),
            pl.BlockSpec((1, 1, ADA_TILE), lambda l, j: (l, 0, j)),
        ],
        out_specs=pl.BlockSpec((1, BATCH, ADA_TILE), lambda l, j: (l, 0, j)),
        out_shape=jax.ShapeDtypeStruct((DEPTH, BATCH, n), F32),
        compiler_params=pltpu.CompilerParams(
            dimension_semantics=("arbitrary", "arbitrary"), vmem_limit_bytes=VMEM_LIMIT_BYTES),
        name="adaln_mod",
    )(c, w_ada, b_ada.reshape(DEPTH, 1, n))


def _mixer_kernel(layer, chunk_decay,
                  x_ref, mod_ref, nw_ref, wqk_ref, win_ref, cos_ref, sin_ref, inner_ref, cross_ref, sdec_ref,
                  mstack_ref, lmask_ref, lbs_ref, retw_ref, hgw_ref, wout_ref,
                  o_ref, act_ref, dec_ref, mix_ref, r_ref, st_ref):
    @pl.when(pl.program_id(1) == 0)
    def _():
        r_ref[...] = jnp.zeros_like(r_ref)
        st_ref[...] = jnp.zeros_like(st_ref)

    x = x_ref[0]
    mod = mod_ref[0]
    sh1, sc1, g1 = mod[0:1], mod[1:2], mod[2:3]
    h = (_rms(x, nw_ref[...]) * (1.0 + sc1) + sh1).astype(BF16)

    def proj(off, width):
        w_ref = wqk_ref if off < OFF_RV else win_ref
        return _dot(h, w_ref[:, off:off + width])

    n_chunks = SEQ_TILE // HG_CHUNK
    cos, sin = cos_ref[...], sin_ref[...]
    k_scale = RET_QK_DIM ** -0.5

    hf = proj(OFF_HF, HG_WIDTH)
    e = jnp.exp(-jnp.abs(hf))
    if layer == 0:
        log_f = jnp.minimum(hf, 0.0) - jnp.log1p(e)
        k_in = jnp.where(hf >= 0.0, e, 1.0) / (1.0 + e)
    else:
        lbs = lbs_ref[...]
        ex = jnp.exp(lbs - jnp.max(lbs, axis=0, keepdims=True))
        probs = ex / jnp.sum(ex, axis=0, keepdims=True)
        lb = jnp.sum(probs[1:layer + 1], axis=0, keepdims=True)
        sig = jnp.where(hf >= 0.0, 1.0, e) / (1.0 + e)
        f = lb + (1.0 - lb) * sig
        log_f = jnp.log(f)
        k_in = 1.0 - f
    act_ref[:, OFF_HF:OFF_HF + HG_WIDTH] = log_f
    act_ref[:, OFF_KIN:OFF_KIN + HG_WIDTH] = k_in
    act_ref[:, OFF_HQ:OFF_HQ + HG_WIDTH] = _silu(proj(OFF_HQ, HG_WIDTH))
    act_ref[:, OFF_HI:OFF_HI + HG_WIDTH] = proj(OFF_HI, HG_WIDTH)

    for ci in range(n_chunks):
        g = act_ref[ci * HG_CHUNK:(ci + 1) * HG_CHUNK, OFF_HF:OFF_HF + HG_WIDTH]
        g_hi = g.astype(BF16)
        g_lo = (g - g_hi.astype(F32)).astype(BF16)
        expo = _dot(mstack_ref[...], jnp.concatenate([g_hi, g_lo], axis=0))
        dec_ref[ci] = jnp.exp(expo)

    def rotary(off, scale):
        p = proj(off, RET_QK_WIDTH)
        p1, p2 = p[:, :LANES], p[:, LANES:]
        act_ref[:, off:off + LANES] = (p1 * cos - p2 * sin) * scale
        act_ref[:, off + LANES:off + 2 * LANES] = (p1 * sin + p2 * cos) * scale

    rotary(OFF_RQ, 1.0)
    rotary(OFF_RK, k_scale)
    act_ref[:, OFF_RV:OFF_RV + RET_WIDTH] = proj(OFF_RV, RET_WIDTH)

    pair_mask = [lmask_ref[lv] > 0.5 for lv in range(N_LEVELS + 1)]
    hg_scores, hg_upd = {}, {}
    for ci in range(n_chunks):
        rows = slice(ci * HG_CHUNK, (ci + 1) * HG_CHUNK)
        for hh in range(HG_HEADS):
            hs = slice(hh * HG_DIM, (hh + 1) * HG_DIM)
            qf = act_ref[rows, OFF_HQ + hs.start:OFF_HQ + hs.stop]
            kk = act_ref[rows, OFF_KIN + hs.start:OFF_KIN + hs.stop]
            scores = jnp.where(pair_mask[0], _dot_nt(qf.astype(BF16), kk.astype(BF16)), 0.0)
            for lv in range(1, N_LEVELS + 1):
                d_lv = dec_ref[ci, lv * HG_CHUNK:(lv + 1) * HG_CHUNK, hs]
                pair = _dot_nt((qf * d_lv).astype(BF16), (kk * d_lv).astype(BF16))
                scores = jnp.where(pair_mask[lv], pair, scores)
            hg_scores[ci, hh] = scores.astype(BF16)
    for ci in range(n_chunks):
        rows = slice(ci * HG_CHUNK, (ci + 1) * HG_CHUNK)
        for hh in range(HG_HEADS):
            hs = slice(hh * HG_DIM, (hh + 1) * HG_DIM)
            kk = act_ref[rows, OFF_KIN + hs.start:OFF_KIN + hs.stop]
            v = act_ref[rows, OFF_HI + hs.start:OFF_HI + hs.stop].astype(BF16)
            d_end = dec_ref[ci, (N_EXPONENTS - 1) * HG_CHUNK:, hs]
            hg_upd[ci, hh] = _dot_tn(v, (kk * d_end).astype(BF16))

    lane = lax.broadcasted_iota(jnp.int32, (1, RET_QK_WIDTH), 1)
    head_of_lane = (lane % LANES) // (RET_QK_DIM // 2)
    n_ret = SEQ_TILE // RET_CHUNK
    ret_scores, ret_upd = {}, {}
    for rc in range(n_ret):
        rows = slice(rc * RET_CHUNK, (rc + 1) * RET_CHUNK)
        q = act_ref[rows, OFF_RQ:OFF_RQ + RET_QK_WIDTH].astype(BF16)
        kf = act_ref[rows, OFF_RK:OFF_RK + RET_QK_WIDTH]
        for hh in range(RET_HEADS):
            k_h = jnp.where(head_of_lane == hh, kf, 0.0)
            v_h = act_ref[rows, OFF_RV + hh * RET_V_DIM:OFF_RV + (hh + 1) * RET_V_DIM].astype(BF16)
            ret_scores[rc, hh] = (_dot_nt(q, k_h.astype(BF16)) * inner_ref[hh]).astype(BF16)
            ret_upd[rc, hh] = _dot_tn((k_h * sdec_ref[hh]).astype(BF16), v_h)

    act_ref[:, OFF_RG:OFF_RG + RET_WIDTH] = _silu(proj(OFF_RG, RET_WIDTH))
    act_ref[:, OFF_HG:OFF_HG + HG_WIDTH] = _silu(proj(OFF_HG, HG_WIDTH))

    for hh in range(HG_HEADS):
        hs = slice(hh * HG_DIM, (hh + 1) * HG_DIM)
        st = st_ref[hh]
        for ci in range(n_chunks):
            rows = slice(ci * HG_CHUNK, (ci + 1) * HG_CHUNK)
            qf = act_ref[rows, OFF_HQ + hs.start:OFF_HQ + hs.stop]
            v = act_ref[rows, OFF_HI + hs.start:OFF_HI + hs.stop].astype(BF16)
            d_cum = dec_ref[ci, 0:HG_CHUNK, hs]
            o = (_dot(hg_scores[ci, hh], v)
                 + _dot_nt((qf * d_cum).astype(BF16), st.astype(BF16)))
            st = st * d_cum[HG_CHUNK - 1:HG_CHUNK] + hg_upd[ci, hh]
            gate = act_ref[rows, OFF_HG + hs.start:OFF_HG + hs.stop]
            mix_ref[rows, RET_WIDTH + hs.start:RET_WIDTH + hs.stop] = _rms(o, hgw_ref[:, hs]) * gate
        st_ref[hh] = st
    for hh in range(RET_HEADS):
        vs = slice(hh * RET_V_DIM, (hh + 1) * RET_V_DIM)
        r_h = r_ref[hh]
        for rc in range(n_ret):
            rows = slice(rc * RET_CHUNK, (rc + 1) * RET_CHUNK)
            q = act_ref[rows, OFF_RQ:OFF_RQ + RET_QK_WIDTH].astype(BF16)
            v_h = act_ref[rows, OFF_RV + vs.start:OFF_RV + vs.stop].astype(BF16)
            o = _dot(ret_scores[rc, hh], v_h) + _dot(q, r_h.astype(BF16)) * cross_ref[hh]
            r_h = chunk_decay[hh] * r_h + ret_upd[rc, hh]
            gate = act_ref[rows, OFF_RG + vs.start:OFF_RG + vs.stop]
            mix_ref[rows, vs] = _rms(o, retw_ref[:, vs]) * gate
        r_ref[hh] = r_h

    mixed = (_dot(mix_ref[:, RET_WIDTH:].astype(BF16), wout_ref[RET_WIDTH:, :])
             + _dot(mix_ref[:, :RET_WIDTH].astype(BF16), wout_ref[:RET_WIDTH, :]))
    o_ref[0] = x + g1 * mixed


def _const_spec(shape):
    zeros = (0,) * len(shape)
    return pl.BlockSpec(shape, lambda b, s: zeros, pipeline_mode=pl.Buffered(1))


def _layer_spec(layer, shape):
    idx = (layer,) + (0,) * len(shape)
    return pl.BlockSpec((None,) + tuple(shape), lambda b, s: idx, pipeline_mode=pl.Buffered(1))


def _mixer_call(layer, x, mod, norm_w, w_qk, w_in, cos, sin, ret_tabs, hg_tabs, lbs, ret_w, hg_w, w_out):
    inner, cross_b, state_b, chunk_decay = ret_tabs
    mstack2, lmask = hg_tabs
    n_exp_rows = N_EXPONENTS * HG_CHUNK
    return pl.pallas_call(
        functools.partial(_mixer_kernel, layer, chunk_decay),
        grid=(BATCH, SEQ // SEQ_TILE),
        in_specs=[
            pl.BlockSpec((1, SEQ_TILE, D_MODEL), lambda b, s: (b, s, 0)),
            pl.BlockSpec((None, 1, N_MOD, D_MODEL), lambda b, s: (layer, b, 0, 0)),
            _layer_spec(layer, (1, D_MODEL)),
            _layer_spec(layer, (D_MODEL, 2 * RET_QK_WIDTH)),
            _layer_spec(layer, (D_MODEL, IN_WIDTH)),
            pl.BlockSpec((SEQ_TILE, LANES), lambda b, s: (s, 0)),
            pl.BlockSpec((SEQ_TILE, LANES), lambda b, s: (s, 0)),
            _const_spec((RET_HEADS, RET_CHUNK, RET_CHUNK)),
            _const_spec((RET_HEADS, RET_CHUNK, RET_V_DIM)),
            _const_spec((RET_HEADS, RET_CHUNK, RET_QK_WIDTH)),
            _const_spec((n_exp_rows, 2 * HG_CHUNK)),
            _const_spec((N_LEVELS + 1, HG_CHUNK, HG_CHUNK)),
            _const_spec((DEPTH, HG_WIDTH)),
            _layer_spec(layer, (1, RET_WIDTH)),
            _layer_spec(layer, (1, HG_WIDTH)),
            _layer_spec(layer, (D_MODEL, D_MODEL)),
        ],
        out_specs=pl.BlockSpec((1, SEQ_TILE, D_MODEL), lambda b, s: (b, s, 0)),
        out_shape=jax.ShapeDtypeStruct((BATCH, SEQ, D_MODEL), F32),
        scratch_shapes=[
            pltpu.VMEM((SEQ_TILE, ACT_WIDTH), F32),
            pltpu.VMEM((SEQ_TILE // HG_CHUNK, N_EXPONENTS * HG_CHUNK, HG_WIDTH), F32),
            pltpu.VMEM((SEQ_TILE, D_MODEL), F32),
            pltpu.VMEM((RET_HEADS, RET_QK_WIDTH, RET_V_DIM), F32),
            pltpu.VMEM((HG_HEADS, HG_DIM, HG_DIM), F32),
        ],
        compiler_params=pltpu.CompilerParams(
            dimension_semantics=("arbitrary", "arbitrary"), vmem_limit_bytes=VMEM_LIMIT_BYTES),
        name=f"mixer_l{layer}",
    )(x, mod, norm_w, w_qk, w_in, cos, sin,
      jnp.asarray(inner), jnp.asarray(cross_b), jnp.asarray(state_b),
      jnp.asarray(mstack2, dtype=BF16), jnp.asarray(lmask),
      lbs, ret_w, hg_w, w_out)


def _ffn_kernel(final, x_ref, mod_ref, nw_ref, wg_ref, wu_ref, wd_ref, fw_ref, o_ref):
    mod = mod_ref[0]
    sh2, sc2, g2 = mod[3:4], mod[4:5], mod[5:6]
    for r in range(FFN_TILE // FFN_ROWS):
        rows = slice(r * FFN_ROWS, (r + 1) * FFN_ROWS)
        x = x_ref[0, rows, :]
        h = (_rms(x, nw_ref[...]) * (1.0 + sc2) + sh2).astype(BF16)
        acc = jnp.zeros((FFN_ROWS, D_MODEL), F32)
        for j in range(D_FF // FFN_CHUNK):
            cs = slice(j * FFN_CHUNK, (j + 1) * FFN_CHUNK)
            a = _silu(_dot(h, wg_ref[:, cs])) * _dot(h, wu_ref[:, cs])
            acc = acc + _dot(a.astype(BF16), wd_ref[cs, :])
        y = x + g2 * acc
        if final:
            y = _rms(y, fw_ref[...])
        o_ref[0, rows, :] = y


def _ffn_call(layer, x, mod, norm_w, w_gate, w_up, w_down, final_w):
    final = layer == DEPTH - 1
    return pl.pallas_call(
        functools.partial(_ffn_kernel, final),
        grid=(BATCH, SEQ // FFN_TILE),
        in_specs=[
            pl.BlockSpec((1, FFN_TILE, D_MODEL), lambda b, s: (b, s, 0)),
            pl.BlockSpec((None, 1, N_MOD, D_MODEL), lambda b, s: (layer, b, 0, 0)),
            _layer_spec(layer, (1, D_MODEL)),
            _layer_spec(layer, (D_MODEL, D_FF)),
            _layer_spec(layer, (D_MODEL, D_FF)),
            _layer_spec(layer, (D_FF, D_MODEL)),
            _const_spec((1, D_MODEL)),
        ],
        out_specs=pl.BlockSpec((1, FFN_TILE, D_MODEL), lambda b, s: (b, s, 0)),
        out_shape=jax.ShapeDtypeStruct((BATCH, SEQ, D_MODEL), F32),
        compiler_params=pltpu.CompilerParams(
            dimension_semantics=("arbitrary", "arbitrary"), vmem_limit_bytes=VMEM_LIMIT_BYTES),
        name="swiglu_final" if final else "swiglu",
    )(x, mod, norm_w, w_gate, w_up, w_down, final_w.reshape(1, D_MODEL))


@jax.jit
def kernel(x, c, w_ada, b_ada, norm_mix_w, w_in, ret_norm_w, hg_lower_bounds, hg_norm_w, w_out,
           norm_ffn_w, w_ffn_gate, w_ffn_up, w_ffn_down, final_norm_w):
    assert x.shape == (BATCH, SEQ, D_MODEL) and x.dtype == F32
    mod_all = _ada_call(c, w_ada, b_ada).reshape(DEPTH, BATCH, N_MOD, D_MODEL)
    cos, sin = _rotary_tables()
    ret_tabs = _ret_tables(RET_CHUNK)
    hg_tabs = _hg_tables()
    w_in_b = w_in.astype(BF16)
    w_qk = _deinterleave_qk(w_in_b[:, :, :2 * RET_QK_WIDTH])
    w_out_b = w_out.astype(BF16)
    w_gate_b, w_up_b, w_down_b = (w.astype(BF16) for w in (w_ffn_gate, w_ffn_up, w_ffn_down))
    norm_mix = norm_mix_w.reshape(DEPTH, 1, D_MODEL)
    norm_ffn = norm_ffn_w.reshape(DEPTH, 1, D_MODEL)
    ret_w = ret_norm_w.reshape(DEPTH, 1, RET_WIDTH)
    hg_w = hg_norm_w.reshape(DEPTH, 1, HG_WIDTH)
    for layer in range(DEPTH):
        x = _mixer_call(layer, x, mod_all, norm_mix, w_qk, w_in_b, cos, sin, ret_tabs, hg_tabs,
                        hg_lower_bounds, ret_w, hg_w, w_out_b)
        x = _ffn_call(layer, x, mod_all, norm_ffn, w_gate_b, w_up_b, w_down_b, final_norm_w)
    return x
```

```python
import functools

import numpy as np
import jax
import jax.numpy as jnp
from jax import lax
from jax.experimental import pallas as pl
from jax.experimental.pallas import tpu as pltpu

D_MODEL = 1024
BATCH = 8
SEQ = 2048
DEPTH = 2
RET_WIDTH = 512
HG_WIDTH = 512
RET_HEADS = 4
RET_V_DIM = 128
RET_QK_DIM = 64
RET_QK_WIDTH = 256
HG_HEADS = 4
HG_DIM = 128
D_FF = 2816
ROPE_BASE = 10000.0
EPS = 1e-6
N_MOD = 6
IN_WIDTH = 3584

OFF_RQ, OFF_RK, OFF_RV, OFF_RG, OFF_HQ, OFF_HF, OFF_HI, OFF_HG = (
    0, 256, 512, 1024, 1536, 2048, 2560, 3072)
OFF_KIN = IN_WIDTH
ACT_WIDTH = IN_WIDTH + HG_WIDTH

LANES = 128
PROJ_COLS = 256
SEQ_TILE = 512
RET_CHUNK = 256
HG_CHUNK = 64
FFN_TILE = 1024
FFN_ROWS = 512
FFN_CHUNK = 256
ADA_TILE = 1536
VMEM_LIMIT_BYTES = 56 * 1024 * 1024

F32 = jnp.float32
BF16 = jnp.bfloat16


def _hg_levels():
    out, m = [], HG_CHUNK // 2
    while m >= 1:
        out.append(m)
        m //= 2
    return out


HG_LEVELS = _hg_levels()
N_LEVELS = len(HG_LEVELS)
HG_HALF = HG_CHUNK // 2
HG_DIRECT_LIMIT = 60.0


def _hg_tables():
    c = HG_CHUNK
    r = np.arange(c)
    u = r[None, :]
    half_start = (r // HG_HALF) * HG_HALF
    half_end = half_start + HG_HALF - 1
    m_q = (u >= half_start[:, None]) & (u <= r[:, None])
    m_e = (u > r[:, None]) & (u <= half_end[:, None])
    half_tab = np.concatenate([m_q, m_e], axis=0).astype(np.float32)
    upper = r >= HG_HALF
    across = upper[:, None] & (~upper)[None, :]
    inside = (upper[:, None] == upper[None, :]) & (r[None, :] <= r[:, None])
    direct_mask = np.concatenate([across, inside], axis=1).astype(np.float32)

    mats, masks = [], [np.eye(c, dtype=bool)]
    for m in HG_LEVELS:
        blk, pos = r // (2 * m), r % (2 * m)
        mid = blk * 2 * m + m
        up = pos >= m
        mats.append(np.where(up[:, None], (u >= mid[:, None]) & (u <= r[:, None]),
                             (u > r[:, None]) & (u < mid[:, None])))
        masks.append((blk[:, None] == blk[None, :]) & up[:, None] & (~up)[None, :])
    level_tab = np.concatenate(mats, axis=0).astype(np.float32)
    level_mask = np.stack([np.concatenate([mk, np.zeros_like(mk)], axis=1) for mk in masks]).astype(np.float32)
    dup = lambda t: np.concatenate([t, t], axis=1)
    return dup(half_tab), direct_mask, dup(level_tab), level_mask


def _ret_tables(chunk):
    h = np.arange(RET_HEADS, dtype=np.float32)
    log_gamma = np.log(np.float32(1.0) - np.power(np.float32(2.0), np.float32(-5.0) - h)).astype(np.float32)
    idx = np.arange(chunk, dtype=np.float32)
    rel = idx[:, None] - idx[None, :]
    inner = np.exp(np.where(rel[None] >= 0, log_gamma[:, None, None] * rel[None], -np.inf)).astype(np.float32)
    cross = np.exp(log_gamma[:, None] * (idx[None, :] + np.float32(1.0))).astype(np.float32)
    state = np.exp(log_gamma[:, None] * (np.float32(chunk) - np.float32(1.0) - idx[None, :])).astype(np.float32)
    chunk_decay = np.exp(log_gamma * np.float32(chunk)).astype(np.float32)
    cross_b = np.broadcast_to(cross[:, :, None], (RET_HEADS, chunk, RET_V_DIM)).copy()
    state_b = np.broadcast_to(state[:, :, None], (RET_HEADS, chunk, RET_QK_WIDTH)).copy()
    return inner, cross_b, state_b, [float(v) for v in chunk_decay]


def _rotary_tables():
    half = RET_QK_DIM // 2
    inv = ROPE_BASE ** (-jnp.linspace(0.0, 1.0, half, dtype=F32))
    pos = jnp.arange(SEQ, dtype=F32)
    theta = pos[:, None] * inv[None, :]
    return jnp.tile(jnp.cos(theta), (1, RET_HEADS)), jnp.tile(jnp.sin(theta), (1, RET_HEADS))


def _deinterleave_qk(w_qk):
    half = RET_QK_DIM // 2
    lead = w_qk.shape[:-1]
    w = w_qk.reshape(lead + (2, RET_HEADS, half, 2))
    w = jnp.moveaxis(w, -1, -3)
    return w.reshape(lead + (2 * RET_QK_WIDTH,))


def _rms(x, w):
    return x * lax.rsqrt(jnp.mean(x * x, axis=-1, keepdims=True) + EPS) * w


def _silu(x):
    return x / (1.0 + jnp.exp(-x))


def _dot(a, b):
    return jnp.dot(a, b, preferred_element_type=F32)


def _dot_nt(a, b):
    return lax.dot_general(a, b, (((1,), (1,)), ((), ())), preferred_element_type=F32)


def _dot_tn(a, b):
    return lax.dot_general(a, b, (((0,), (0,)), ((), ())), preferred_element_type=F32)


def _ada_kernel(c_ref, w_ref, b_ref, o_ref):
    c_act = _silu(c_ref[...])
    o_ref[0] = _dot(c_act.astype(BF16), w_ref[0].astype(BF16)) + b_ref[0]


def _ada_call(c, w_ada, b_ada):
    n = N_MOD * D_MODEL
    return pl.pallas_call(
        _ada_kernel,
        grid=(DEPTH, n // ADA_TILE),
        in_specs=[
            pl.BlockSpec((BATCH, D_MODEL), lambda l, j: (0, 0)),
            pl.BlockSpec((1, D_MODEL, ADA_TILE), lambda l, j: (l, 0, j)),
            pl.BlockSpec((1, 1, ADA_TILE), lambda l, j: (l, 0, j)),
        ],
        out_specs=pl.BlockSpec((1, BATCH, ADA_TILE), lambda l, j: (l, 0, j)),
        out_shape=jax.ShapeDtypeStruct((DEPTH, BATCH, n), F32),
        compiler_params=pltpu.CompilerParams(
            dimension_semantics=("arbitrary", "arbitrary"), vmem_limit_bytes=VMEM_LIMIT_BYTES),
        name="adaln_mod",
    )(c, w_ada, b_ada.reshape(DEPTH, 1, n))


def _mixer_kernel(layer, chunk_decay,
                  x_ref, mod_ref, nw_ref, wqk_ref, win_ref, cos_ref, sin_ref, inner_ref, cross_ref, sdec_ref,
                  htab_ref, dmask_ref, ltab_ref, lmask_ref, lbs_ref, retw_ref, hgw_ref, wout_ref,
                  o_ref, act_ref, dec_ref, lvl_ref, sc_ref, mix_ref, r_ref, st_ref):
    @pl.when(pl.program_id(1) == 0)
    def _():
        r_ref[...] = jnp.zeros_like(r_ref)
        st_ref[...] = jnp.zeros_like(st_ref)

    x = x_ref[0]
    mod = mod_ref[0]
    sh1, sc1, g1 = mod[0:1], mod[1:2], mod[2:3]
    h = (_rms(x, nw_ref[...]) * (1.0 + sc1) + sh1).astype(BF16)

    def proj(off, width):
        w_ref = wqk_ref if off < OFF_RV else win_ref
        return _dot(h, w_ref[:, off:off + width])

    n_chunks = SEQ_TILE // HG_CHUNK
    cos, sin = cos_ref[...], sin_ref[...]
    k_scale = RET_QK_DIM ** -0.5

    if layer > 0:
        lbs = lbs_ref[...]
        ex = jnp.exp(lbs - jnp.max(lbs, axis=0, keepdims=True))
        probs = ex / jnp.sum(ex, axis=0, keepdims=True)
        lower_bound = jnp.sum(probs[1:layer + 1], axis=0, keepdims=True)

    def forget_unit(i):
        cols = slice(i * PROJ_COLS, (i + 1) * PROJ_COLS)
        hf = proj(OFF_HF + cols.start, PROJ_COLS)
        e = jnp.exp(-jnp.abs(hf))
        if layer == 0:
            log_f = jnp.minimum(hf, 0.0) - jnp.log(1.0 + e)
            k_in = jnp.where(hf >= 0.0, e, 1.0) / (1.0 + e)
        else:
            lb = lower_bound[:, cols]
            f = lb + (1.0 - lb) * (jnp.where(hf >= 0.0, 1.0, e) / (1.0 + e))
            log_f = jnp.log(f)
            k_in = 1.0 - f
        act_ref[:, OFF_HF + cols.start:OFF_HF + cols.stop] = log_f
        act_ref[:, OFF_KIN + cols.start:OFF_KIN + cols.stop] = k_in

    def plain_unit(off):
        act_ref[:, off:off + PROJ_COLS] = proj(off, PROJ_COLS)

    def silu_unit(off):
        act_ref[:, off:off + PROJ_COLS] = _silu(proj(off, PROJ_COLS))

    def rotary_unit(off, scale):
        p = proj(off, RET_QK_WIDTH)
        p1, p2 = p[:, :LANES], p[:, LANES:]
        act_ref[:, off:off + LANES] = (p1 * cos - p2 * sin) * scale
        act_ref[:, off + LANES:off + 2 * LANES] = (p1 * sin + p2 * cos) * scale

    for i in range(HG_WIDTH // PROJ_COLS):
        forget_unit(i)
        plain_unit(OFF_HI + i * PROJ_COLS)
    for i in range(HG_WIDTH // PROJ_COLS):
        silu_unit(OFF_HQ + i * PROJ_COLS)
        plain_unit(OFF_RV + i * PROJ_COLS)
    rotary_unit(OFF_RQ, 1.0)
    rotary_unit(OFF_RK, k_scale)

    def log_f_split(rows):
        g = act_ref[rows, OFF_HF:OFF_HF + HG_WIDTH]
        g_hi = g.astype(BF16)
        g_lo = (g - g_hi.astype(F32)).astype(BF16)
        return jnp.concatenate([g_hi, g_lo], axis=0)

    e_min = jnp.zeros((HG_CHUNK, HG_WIDTH), F32)
    for ci in range(n_chunks):
        expo = _dot(htab_ref[...], log_f_split(slice(ci * HG_CHUNK, (ci + 1) * HG_CHUNK)))
        e_q = expo[:HG_CHUNK]
        e_min = jnp.minimum(e_min, e_q)
        dec_ref[ci, :2 * HG_CHUNK] = jnp.exp(expo)
        dec_ref[ci, 2 * HG_CHUNK:] = jnp.exp(-e_q)
    direct_ok = jnp.min(e_min) >= -HG_DIRECT_LIMIT

    lane = lax.broadcasted_iota(jnp.int32, (1, RET_QK_WIDTH), 1)
    head_of_lane = (lane % LANES) // (RET_QK_DIM // 2)
    n_ret = SEQ_TILE // RET_CHUNK
    ret_scores, ret_upd = {}, {}
    for rc in range(n_ret):
        rows = slice(rc * RET_CHUNK, (rc + 1) * RET_CHUNK)
        q = act_ref[rows, OFF_RQ:OFF_RQ + RET_QK_WIDTH].astype(BF16)
        kf = act_ref[rows, OFF_RK:OFF_RK + RET_QK_WIDTH]
        for hh in range(RET_HEADS):
            k_h = jnp.where(head_of_lane == hh, kf, 0.0)
            v_h = act_ref[rows, OFF_RV + hh * RET_V_DIM:OFF_RV + (hh + 1) * RET_V_DIM].astype(BF16)
            ret_scores[rc, hh] = (_dot_nt(q, k_h.astype(BF16)) * inner_ref[hh]).astype(BF16)
            ret_upd[rc, hh] = _dot_tn((k_h * sdec_ref[hh]).astype(BF16), v_h)

    @pl.when(direct_ok)
    def _():
        keep = dmask_ref[...] > 0.5
        for ci in range(n_chunks):
            rows = slice(ci * HG_CHUNK, (ci + 1) * HG_CHUNK)
            for hh in range(HG_HEADS):
                hs = slice(hh * HG_DIM, (hh + 1) * HG_DIM)
                qf = act_ref[rows, OFF_HQ + hs.start:OFF_HQ + hs.stop]
                kk = act_ref[rows, OFF_KIN + hs.start:OFF_KIN + hs.stop]
                a_q = dec_ref[ci, 0:HG_CHUNK, hs]
                a_e = dec_ref[ci, HG_CHUNK:2 * HG_CHUNK, hs]
                a_qinv = dec_ref[ci, 2 * HG_CHUNK:, hs]
                keys = jnp.concatenate([(kk * a_e).astype(BF16), (kk * a_qinv).astype(BF16)], axis=0)
                pair = _dot_nt((qf * a_q).astype(BF16), keys)
                sc_ref[ci * HG_HEADS + hh] = jnp.where(keep, pair, 0.0).astype(BF16)

    @pl.when(jnp.logical_not(direct_ok))
    def _():
        def chunk_scores(ci, carry):
            rows = pl.ds(pl.multiple_of(ci * HG_CHUNK, HG_CHUNK), HG_CHUNK)
            lvl_ref[...] = jnp.exp(_dot(ltab_ref[...], log_f_split(rows)))
            no_keys = jnp.zeros((HG_CHUNK, HG_DIM), BF16)
            for hh in range(HG_HEADS):
                hs = slice(hh * HG_DIM, (hh + 1) * HG_DIM)
                qf = act_ref[rows, OFF_HQ + hs.start:OFF_HQ + hs.stop]
                kk = act_ref[rows, OFF_KIN + hs.start:OFF_KIN + hs.stop]
                keys = jnp.concatenate([kk.astype(BF16), no_keys], axis=0)
                scores = jnp.where(lmask_ref[0] > 0.5, _dot_nt(qf.astype(BF16), keys), 0.0)
                for lv in range(N_LEVELS):
                    d_lv = lvl_ref[lv * HG_CHUNK:(lv + 1) * HG_CHUNK, hs]
                    keys = jnp.concatenate([(kk * d_lv).astype(BF16), no_keys], axis=0)
                    pair = _dot_nt((qf * d_lv).astype(BF16), keys)
                    scores = jnp.where(lmask_ref[lv + 1] > 0.5, pair, scores)
                sc_ref[ci * HG_HEADS + hh] = scores.astype(BF16)
            return carry

        lax.fori_loop(0, n_chunks, chunk_scores, 0)

    upper_half = lax.broadcasted_iota(jnp.int32, (HG_CHUNK, HG_DIM), 0) >= HG_HALF

    def chunk_decays(ci, hs):
        a_q = dec_ref[ci, 0:HG_CHUNK, hs]
        a_e = dec_ref[ci, HG_CHUNK:2 * HG_CHUNK, hs]
        first_half = a_q[HG_HALF - 1:HG_HALF]
        second_half = a_q[HG_CHUNK - 1:HG_CHUNK]
        d_cum = a_q * jnp.where(upper_half, first_half, 1.0)
        d_end = a_e * jnp.where(upper_half, 1.0, second_half)
        return d_cum, d_end, first_half * second_half

    hg_upd = {}
    for ci in range(n_chunks):
        rows = slice(ci * HG_CHUNK, (ci + 1) * HG_CHUNK)
        for hh in range(HG_HEADS):
            hs = slice(hh * HG_DIM, (hh + 1) * HG_DIM)
            kk = act_ref[rows, OFF_KIN + hs.start:OFF_KIN + hs.stop]
            v = act_ref[rows, OFF_HI + hs.start:OFF_HI + hs.stop].astype(BF16)
            _, d_end, _ = chunk_decays(ci, hs)
            hg_upd[ci, hh] = _dot_tn(v, (kk * d_end).astype(BF16))

    for i in range(RET_WIDTH // PROJ_COLS):
        silu_unit(OFF_RG + i * PROJ_COLS)
    for i in range(HG_WIDTH // PROJ_COLS):
        silu_unit(OFF_HG + i * PROJ_COLS)

    for hh in range(HG_HEADS):
        hs = slice(hh * HG_DIM, (hh + 1) * HG_DIM)
        st = st_ref[hh]
        for ci in range(n_chunks):
            rows = slice(ci * HG_CHUNK, (ci + 1) * HG_CHUNK)
            qf = act_ref[rows, OFF_HQ + hs.start:OFF_HQ + hs.stop]
            v = act_ref[rows, OFF_HI + hs.start:OFF_HI + hs.stop].astype(BF16)
            d_cum, _, d_all = chunk_decays(ci, hs)
            o = (_dot(sc_ref[ci * HG_HEADS + hh], jnp.concatenate([v, v], axis=0))
                 + _dot_nt((qf * d_cum).astype(BF16), st.astype(BF16)))
            st = st * d_all + hg_upd[ci, hh]
            gate = act_ref[rows, OFF_HG + hs.start:OFF_HG + hs.stop]
            mix_ref[rows, RET_WIDTH + hs.start:RET_WIDTH + hs.stop] = _rms(o, hgw_ref[:, hs]) * gate
        st_ref[hh] = st
    for hh in range(RET_HEADS):
        vs = slice(hh * RET_V_DIM, (hh + 1) * RET_V_DIM)
        r_h = r_ref[hh]
        for rc in range(n_ret):
            rows = slice(rc * RET_CHUNK, (rc + 1) * RET_CHUNK)
            q = act_ref[rows, OFF_RQ:OFF_RQ + RET_QK_WIDTH].astype(BF16)
            v_h = act_ref[rows, OFF_RV + vs.start:OFF_RV + vs.stop].astype(BF16)
            o = _dot(ret_scores[rc, hh], v_h) + _dot(q, r_h.astype(BF16)) * cross_ref[hh]
            r_h = chunk_decay[hh] * r_h + ret_upd[rc, hh]
            gate = act_ref[rows, OFF_RG + vs.start:OFF_RG + vs.stop]
            mix_ref[rows, vs] = _rms(o, retw_ref[:, vs]) * gate
        r_ref[hh] = r_h

    mixed = (_dot(mix_ref[:, RET_WIDTH:].astype(BF16), wout_ref[RET_WIDTH:, :])
             + _dot(mix_ref[:, :RET_WIDTH].astype(BF16), wout_ref[:RET_WIDTH, :]))
    o_ref[0] = x + g1 * mixed


def _const_spec(shape):
    zeros = (0,) * len(shape)
    return pl.BlockSpec(shape, lambda b, s: zeros, pipeline_mode=pl.Buffered(1))


def _layer_spec(layer, shape):
    idx = (layer,) + (0,) * len(shape)
    return pl.BlockSpec((None,) + tuple(shape), lambda b, s: idx, pipeline_mode=pl.Buffered(1))


def _mixer_call(layer, x, mod, norm_w, w_qk, w_in, cos, sin, ret_tabs, hg_tabs, lbs, ret_w, hg_w, w_out):
    inner, cross_b, state_b, chunk_decay = ret_tabs
    half_tab, direct_mask, level_tab, level_mask = hg_tabs
    n_chunks = SEQ_TILE // HG_CHUNK
    return pl.pallas_call(
        functools.partial(_mixer_kernel, layer, chunk_decay),
        grid=(BATCH, SEQ // SEQ_TILE),
        in_specs=[
            pl.BlockSpec((1, SEQ_TILE, D_MODEL), lambda b, s: (b, s, 0)),
            pl.BlockSpec((None, 1, N_MOD, D_MODEL), lambda b, s: (layer, b, 0, 0)),
            _layer_spec(layer, (1, D_MODEL)),
            _layer_spec(layer, (D_MODEL, 2 * RET_QK_WIDTH)),
            _layer_spec(layer, (D_MODEL, IN_WIDTH)),
            pl.BlockSpec((SEQ_TILE, LANES), lambda b, s: (s, 0)),
            pl.BlockSpec((SEQ_TILE, LANES), lambda b, s: (s, 0)),
            _const_spec((RET_HEADS, RET_CHUNK, RET_CHUNK)),
            _const_spec((RET_HEADS, RET_CHUNK, RET_V_DIM)),
            _const_spec((RET_HEADS, RET_CHUNK, RET_QK_WIDTH)),
            _const_spec((2 * HG_CHUNK, 2 * HG_CHUNK)),
            _const_spec((HG_CHUNK, 2 * HG_CHUNK)),
            _const_spec((N_LEVELS * HG_CHUNK, 2 * HG_CHUNK)),
            _const_spec((N_LEVELS + 1, HG_CHUNK, 2 * HG_CHUNK)),
            _const_spec((DEPTH, HG_WIDTH)),
            _layer_spec(layer, (1, RET_WIDTH)),
            _layer_spec(layer, (1, HG_WIDTH)),
            _layer_spec(layer, (D_MODEL, D_MODEL)),
        ],
        out_specs=pl.BlockSpec((1, SEQ_TILE, D_MODEL), lambda b, s: (b, s, 0)),
        out_shape=jax.ShapeDtypeStruct((BATCH, SEQ, D_MODEL), F32),
        scratch_shapes=[
            pltpu.VMEM((SEQ_TILE, ACT_WIDTH), F32),
            pltpu.VMEM((n_chunks, 3 * HG_CHUNK, HG_WIDTH), F32),
            pltpu.VMEM((N_LEVELS * HG_CHUNK, HG_WIDTH), F32),
            pltpu.VMEM((n_chunks * HG_HEADS, HG_CHUNK, 2 * HG_CHUNK), BF16),
            pltpu.VMEM((SEQ_TILE, D_MODEL), F32),
            pltpu.VMEM((RET_HEADS, RET_QK_WIDTH, RET_V_DIM), F32),
            pltpu.VMEM((HG_HEADS, HG_DIM, HG_DIM), F32),
        ],
        compiler_params=pltpu.CompilerParams(
            dimension_semantics=("arbitrary", "arbitrary"), vmem_limit_bytes=VMEM_LIMIT_BYTES),
        name=f"mixer_l{layer}",
    )(x, mod, norm_w, w_qk, w_in, cos, sin,
      jnp.asarray(inner), jnp.asarray(cross_b), jnp.asarray(state_b),
      jnp.asarray(half_tab, dtype=BF16), jnp.asarray(direct_mask),
      jnp.asarray(level_tab, dtype=BF16), jnp.asarray(level_mask),
      lbs, ret_w, hg_w, w_out)


def _ffn_kernel(final, x_ref, mod_ref, nw_ref, wg_ref, wu_ref, wd_ref, fw_ref, o_ref):
    mod = mod_ref[0]
    sh2, sc2, g2 = mod[3:4], mod[4:5], mod[5:6]
    for r in range(FFN_TILE // FFN_ROWS):
        rows = slice(r * FFN_ROWS, (r + 1) * FFN_ROWS)
        x = x_ref[0, rows, :]
        h = (_rms(x, nw_ref[...]) * (1.0 + sc2) + sh2).astype(BF16)
        acc = jnp.zeros((FFN_ROWS, D_MODEL), F32)
        for j in range(D_FF // FFN_CHUNK):
            cs = slice(j * FFN_CHUNK, (j + 1) * FFN_CHUNK)
            a = _silu(_dot(h, wg_ref[:, cs])) * _dot(h, wu_ref[:, cs])
            acc = acc + _dot(a.astype(BF16), wd_ref[cs, :])
        y = x + g2 * acc
        if final:
            y = _rms(y, fw_ref[...])
        o_ref[0, rows, :] = y


def _ffn_call(layer, x, mod, norm_w, w_gate, w_up, w_down, final_w):
    final = layer == DEPTH - 1
    return pl.pallas_call(
        functools.partial(_ffn_kernel, final),
        grid=(BATCH, SEQ // FFN_TILE),
        in_specs=[
            pl.BlockSpec((1, FFN_TILE, D_MODEL), lambda b, s: (b, s, 0)),
            pl.BlockSpec((None, 1, N_MOD, D_MODEL), lambda b, s: (layer, b, 0, 0)),
            _layer_spec(layer, (1, D_MODEL)),
            _layer_spec(layer, (D_MODEL, D_FF)),
            _layer_spec(layer, (D_MODEL, D_FF)),
            _layer_spec(layer, (D_FF, D_MODEL)),
            _const_spec((1, D_MODEL)),
        ],
        out_specs=pl.BlockSpec((1, FFN_TILE, D_MODEL), lambda b, s: (b, s, 0)),
        out_shape=jax.ShapeDtypeStruct((BATCH, SEQ, D_MODEL), F32),
        compiler_params=pltpu.CompilerParams(
            dimension_semantics=("arbitrary", "arbitrary"), vmem_limit_bytes=VMEM_LIMIT_BYTES),
        name="swiglu_final" if final else "swiglu",
    )(x, mod, norm_w, w_gate, w_up, w_down, final_w.reshape(1, D_MODEL))


@jax.jit
def kernel(x, c, w_ada, b_ada, norm_mix_w, w_in, ret_norm_w, hg_lower_bounds, hg_norm_w, w_out,
           norm_ffn_w, w_ffn_gate, w_ffn_up, w_ffn_down, final_norm_w):
    assert x.shape == (BATCH, SEQ, D_MODEL) and x.dtype == F32
    mod_all = _ada_call(c, w_ada, b_ada).reshape(DEPTH, BATCH, N_MOD, D_MODEL)
    cos, sin = _rotary_tables()
    ret_tabs = _ret_tables(RET_CHUNK)
    hg_tabs = _hg_tables()
    w_in_b = w_in.astype(BF16)
    w_qk = _deinterleave_qk(w_in_b[:, :, :2 * RET_QK_WIDTH])
    w_out_b = w_out.astype(BF16)
    w_gate_b, w_up_b, w_down_b = (w.astype(BF16) for w in (w_ffn_gate, w_ffn_up, w_ffn_down))
    norm_mix = norm_mix_w.reshape(DEPTH, 1, D_MODEL)
    norm_ffn = norm_ffn_w.reshape(DEPTH, 1, D_MODEL)
    ret_w = ret_norm_w.reshape(DEPTH, 1, RET_WIDTH)
    hg_w = hg_norm_w.reshape(DEPTH, 1, HG_WIDTH)
    for layer in range(DEPTH):
        x = _mixer_call(layer, x, mod_all, norm_mix, w_qk, w_in_b, cos, sin, ret_tabs, hg_tabs,
                        hg_lower_bounds, ret_w, hg_w, w_out_b)
        x = _ffn_call(layer, x, mod_all, norm_ffn, w_gate_b, w_up_b, w_down_b, final_norm_w)
    return x
```

```python
import functools

import numpy as np
import jax
import jax.numpy as jnp
from jax import lax
from jax.experimental import pallas as pl
from jax.experimental.pallas import tpu as pltpu

D_MODEL = 1024
BATCH = 8
SEQ = 2048
DEPTH = 2
RET_WIDTH = 512
HG_WIDTH = 512
RET_HEADS = 4
RET_V_DIM = 128
RET_QK_DIM = 64
RET_QK_WIDTH = 256
HG_HEADS = 4
HG_DIM = 128
D_FF = 2816
ROPE_BASE = 10000.0
EPS = 1e-6
N_MOD = 6
IN_WIDTH = 3584

OFF_RQ, OFF_RK, OFF_RV, OFF_RG, OFF_HQ, OFF_HF, OFF_HI, OFF_HG = (
    0, 256, 512, 1024, 1536, 2048, 2560, 3072)
OFF_KIN = IN_WIDTH
ACT_WIDTH = IN_WIDTH + HG_WIDTH

LANES = 128
PROJ_COLS = 256
SEQ_TILE = 512
RET_CHUNK = 256
HG_CHUNK = 64
FFN_TILE = 1024
FFN_ROWS = 512
FFN_CHUNK = 256
ADA_TILE = 1536
VMEM_LIMIT_BYTES = 56 * 1024 * 1024

F32 = jnp.float32
BF16 = jnp.bfloat16


def _hg_levels():
    out, m = [], HG_CHUNK // 2
    while m >= 1:
        out.append(m)
        m //= 2
    return out


HG_LEVELS = _hg_levels()
N_LEVELS = len(HG_LEVELS)
HG_HALF = HG_CHUNK // 2
HG_DIRECT_LIMIT = 60.0


def _hg_tables():
    c = HG_CHUNK
    r = np.arange(c)
    u = r[None, :]
    half_start = (r // HG_HALF) * HG_HALF
    half_end = half_start + HG_HALF - 1
    m_q = (u >= half_start[:, None]) & (u <= r[:, None])
    m_e = (u > r[:, None]) & (u <= half_end[:, None])
    half_tab = np.concatenate([m_q, m_e], axis=0).astype(np.float32)
    upper = r >= HG_HALF
    across = upper[:, None] & (~upper)[None, :]
    inside = (upper[:, None] == upper[None, :]) & (r[None, :] <= r[:, None])
    direct_mask = np.concatenate([across, inside], axis=1).astype(np.float32)

    mats, masks = [], [np.eye(c, dtype=bool)]
    for m in HG_LEVELS:
        blk, pos = r // (2 * m), r % (2 * m)
        mid = blk * 2 * m + m
        up = pos >= m
        mats.append(np.where(up[:, None], (u >= mid[:, None]) & (u <= r[:, None]),
                             (u > r[:, None]) & (u < mid[:, None])))
        masks.append((blk[:, None] == blk[None, :]) & up[:, None] & (~up)[None, :])
    level_tab = np.concatenate(mats, axis=0).astype(np.float32)
    level_mask = np.stack([np.concatenate([mk, np.zeros_like(mk)], axis=1) for mk in masks]).astype(np.float32)
    dup = lambda t: np.concatenate([t, t], axis=1)
    return dup(half_tab), direct_mask, dup(level_tab), level_mask


def _ret_tables(chunk):
    h = np.arange(RET_HEADS, dtype=np.float32)
    log_gamma = np.log(np.float32(1.0) - np.power(np.float32(2.0), np.float32(-5.0) - h)).astype(np.float32)
    idx = np.arange(chunk, dtype=np.float32)
    rel = idx[:, None] - idx[None, :]
    inner = np.exp(np.where(rel[None] >= 0, log_gamma[:, None, None] * rel[None], -np.inf)).astype(np.float32)
    cross = np.exp(log_gamma[:, None] * (idx[None, :] + np.float32(1.0))).astype(np.float32)
    state = np.exp(log_gamma[:, None] * (np.float32(chunk) - np.float32(1.0) - idx[None, :])).astype(np.float32)
    chunk_decay = np.exp(log_gamma * np.float32(chunk)).astype(np.float32)
    cross_b = np.broadcast_to(cross[:, :, None], (RET_HEADS, chunk, RET_V_DIM)).copy()
    state_b = np.broadcast_to(state[:, :, None], (RET_HEADS, chunk, LANES)).copy()
    return inner, cross_b, state_b, [float(v) for v in chunk_decay]


RET_HEADS_PER_GROUP = LANES // RET_QK_DIM


def _rotary_tables():
    half = RET_QK_DIM // 2
    inv = ROPE_BASE ** (-jnp.linspace(0.0, 1.0, half, dtype=F32))
    pos = jnp.arange(SEQ, dtype=F32)
    theta = pos[:, None] * inv[None, :]
    sin = jnp.tile(jnp.sin(theta), (1, RET_HEADS_PER_GROUP))
    return jnp.tile(jnp.cos(theta), (1, 2 * RET_HEADS_PER_GROUP)), jnp.concatenate([-sin, sin], axis=1)


def _deinterleave_qk(w_qk):
    half = RET_QK_DIM // 2
    lead = w_qk.shape[:-1]
    w = w_qk.reshape(lead + (2, RET_HEADS // RET_HEADS_PER_GROUP, RET_HEADS_PER_GROUP, half, 2))
    w = jnp.moveaxis(w, -1, -3)
    return w.reshape(lead + (2 * RET_QK_WIDTH,))


def _rms(x, w):
    return x * lax.rsqrt(jnp.mean(x * x, axis=-1, keepdims=True) + EPS) * w


def _silu(x):
    return x / (1.0 + jnp.exp(-x))


def _dot(a, b):
    return jnp.dot(a, b, preferred_element_type=F32)


def _dot_nt(a, b):
    return lax.dot_general(a, b, (((1,), (1,)), ((), ())), preferred_element_type=F32)


def _dot_tn(a, b):
    return lax.dot_general(a, b, (((0,), (0,)), ((), ())), preferred_element_type=F32)


def _ada_kernel(c_ref, w_ref, b_ref, o_ref):
    c_act = _silu(c_ref[...])
    o_ref[0] = _dot(c_act.astype(BF16), w_ref[0].astype(BF16)) + b_ref[0]


def _ada_call(c, w_ada, b_ada):
    n = N_MOD * D_MODEL
    return pl.pallas_call(
        _ada_kernel,
        grid=(DEPTH, n // ADA_TILE),
        in_specs=[
            pl.BlockSpec((BATCH, D_MODEL), lambda l, j: (0, 0)),
            pl.BlockSpec((1, D_MODEL, ADA_TILE), lambda l, j: (l, 0, j)),
            pl.BlockSpec((1, 1, ADA_TILE), lambda l, j: (l, 0, j)),
        ],
        out_specs=pl.BlockSpec((1, BATCH, ADA_TILE), lambda l, j: (l, 0, j)),
        out_shape=jax.ShapeDtypeStruct((DEPTH, BATCH, n), F32),
        compiler_params=pltpu.CompilerParams(
            dimension_semantics=("arbitrary", "arbitrary"), vmem_limit_bytes=VMEM_LIMIT_BYTES),
        name="adaln_mod",
    )(c, w_ada, b_ada.reshape(DEPTH, 1, n))


def _mixer_kernel(layer, chunk_decay,
                  x_ref, mod_ref, nw_ref, wqk_ref, win_ref, cos_ref, sin_ref, inner_ref, cross_ref, sdec_ref,
                  htab_ref, dmask_ref, ltab_ref, lmask_ref, lbs_ref, retw_ref, hgw_ref, wout_ref,
                  o_ref, act_ref, dec_ref, lvl_ref, sc_ref, mix_ref, r_ref, st_ref):
    @pl.when(pl.program_id(1) == 0)
    def _():
        r_ref[...] = jnp.zeros_like(r_ref)
        st_ref[...] = jnp.zeros_like(st_ref)

    x = x_ref[0]
    mod = mod_ref[0]
    sh1, sc1, g1 = mod[0:1], mod[1:2], mod[2:3]
    h = (_rms(x, nw_ref[...]) * (1.0 + sc1) + sh1).astype(BF16)

    def proj(off, width):
        w_ref = wqk_ref if off < OFF_RV else win_ref
        return _dot(h, w_ref[:, off:off + width])

    n_chunks = SEQ_TILE // HG_CHUNK
    cos, sin = cos_ref[...], sin_ref[...]
    k_scale = RET_QK_DIM ** -0.5

    if layer > 0:
        lbs = lbs_ref[...]
        ex = jnp.exp(lbs - jnp.max(lbs, axis=0, keepdims=True))
        probs = ex / jnp.sum(ex, axis=0, keepdims=True)
        lower_bound = jnp.sum(probs[1:layer + 1], axis=0, keepdims=True)

    def forget_unit(i):
        cols = slice(i * PROJ_COLS, (i + 1) * PROJ_COLS)
        hf = proj(OFF_HF + cols.start, PROJ_COLS)
        e = jnp.exp(-jnp.abs(hf))
        if layer == 0:
            log_f = jnp.minimum(hf, 0.0) - jnp.log(1.0 + e)
            k_in = jnp.where(hf >= 0.0, e, 1.0) / (1.0 + e)
        else:
            lb = lower_bound[:, cols]
            f = lb + (1.0 - lb) * (jnp.where(hf >= 0.0, 1.0, e) / (1.0 + e))
            log_f = jnp.log(f)
            k_in = 1.0 - f
        act_ref[:, OFF_HF + cols.start:OFF_HF + cols.stop] = log_f
        act_ref[:, OFF_KIN + cols.start:OFF_KIN + cols.stop] = k_in

    def plain_unit(off):
        act_ref[:, off:off + PROJ_COLS] = proj(off, PROJ_COLS)

    def silu_unit(off):
        act_ref[:, off:off + PROJ_COLS] = _silu(proj(off, PROJ_COLS))

    def rotary_unit(off, scale):
        p = proj(off, RET_QK_WIDTH)
        for grp in range(RET_QK_WIDTH // LANES):
            pg = p[:, grp * LANES:(grp + 1) * LANES]
            swapped = pltpu.roll(pg, LANES // 2, 1)
            act_ref[:, off + grp * LANES:off + (grp + 1) * LANES] = (pg * cos + swapped * sin) * scale

    for i in range(HG_WIDTH // PROJ_COLS):
        forget_unit(i)
        plain_unit(OFF_HI + i * PROJ_COLS)
    for i in range(HG_WIDTH // PROJ_COLS):
        silu_unit(OFF_HQ + i * PROJ_COLS)
        plain_unit(OFF_RV + i * PROJ_COLS)
    rotary_unit(OFF_RQ, 1.0)
    rotary_unit(OFF_RK, k_scale)

    def log_f_split(rows):
        g = act_ref[rows, OFF_HF:OFF_HF + HG_WIDTH]
        g_hi = g.astype(BF16)
        g_lo = (g - g_hi.astype(F32)).astype(BF16)
        return jnp.concatenate([g_hi, g_lo], axis=0)

    e_min = jnp.zeros((HG_CHUNK, HG_WIDTH), F32)
    for ci in range(n_chunks):
        expo = _dot(htab_ref[...], log_f_split(slice(ci * HG_CHUNK, (ci + 1) * HG_CHUNK)))
        e_q = expo[:HG_CHUNK]
        e_min = jnp.minimum(e_min, e_q)
        dec_ref[ci, :2 * HG_CHUNK] = jnp.exp(expo)
        dec_ref[ci, 2 * HG_CHUNK:] = jnp.exp(-e_q)
    direct_ok = jnp.min(e_min) >= -HG_DIRECT_LIMIT

    lane = lax.broadcasted_iota(jnp.int32, (1, LANES), 1)
    head_in_group = (lane % (LANES // 2)) // (RET_QK_DIM // 2)
    n_ret = SEQ_TILE // RET_CHUNK

    def group_cols(off, hh):
        start = off + (hh // RET_HEADS_PER_GROUP) * LANES
        return slice(start, start + LANES)

    ret_scores, ret_upd = {}, {}
    for rc in range(n_ret):
        rows = slice(rc * RET_CHUNK, (rc + 1) * RET_CHUNK)
        for hh in range(RET_HEADS):
            q = act_ref[rows, group_cols(OFF_RQ, hh)].astype(BF16)
            kf = act_ref[rows, group_cols(OFF_RK, hh)]
            k_h = jnp.where(head_in_group == hh % RET_HEADS_PER_GROUP, kf, 0.0)
            v_h = act_ref[rows, OFF_RV + hh * RET_V_DIM:OFF_RV + (hh + 1) * RET_V_DIM].astype(BF16)
            ret_scores[rc, hh] = (_dot_nt(q, k_h.astype(BF16)) * inner_ref[hh]).astype(BF16)
            ret_upd[rc, hh] = _dot_tn((k_h * sdec_ref[hh]).astype(BF16), v_h)

    @pl.when(direct_ok)
    def _():
        keep = dmask_ref[...] > 0.5
        for ci in range(n_chunks):
            rows = slice(ci * HG_CHUNK, (ci + 1) * HG_CHUNK)
            for hh in range(HG_HEADS):
                hs = slice(hh * HG_DIM, (hh + 1) * HG_DIM)
                qf = act_ref[rows, OFF_HQ + hs.start:OFF_HQ + hs.stop]
                kk = act_ref[rows, OFF_KIN + hs.start:OFF_KIN + hs.stop]
                a_q = dec_ref[ci, 0:HG_CHUNK, hs]
                a_e = dec_ref[ci, HG_CHUNK:2 * HG_CHUNK, hs]
                a_qinv = dec_ref[ci, 2 * HG_CHUNK:, hs]
                keys = jnp.concatenate([(kk * a_e).astype(BF16), (kk * a_qinv).astype(BF16)], axis=0)
                pair = _dot_nt((qf * a_q).astype(BF16), keys)
                sc_ref[ci * HG_HEADS + hh] = jnp.where(keep, pair, 0.0).astype(BF16)

    @pl.when(jnp.logical_not(direct_ok))
    def _():
        def chunk_scores(ci, carry):
            rows = pl.ds(pl.multiple_of(ci * HG_CHUNK, HG_CHUNK), HG_CHUNK)
            lvl_ref[...] = jnp.exp(_dot(ltab_ref[...], log_f_split(rows)))
            no_keys = jnp.zeros((HG_CHUNK, HG_DIM), BF16)
            for hh in range(HG_HEADS):
                hs = slice(hh * HG_DIM, (hh + 1) * HG_DIM)
                qf = act_ref[rows, OFF_HQ + hs.start:OFF_HQ + hs.stop]
                kk = act_ref[rows, OFF_KIN + hs.start:OFF_KIN + hs.stop]
                keys = jnp.concatenate([kk.astype(BF16), no_keys], axis=0)
                scores = jnp.where(lmask_ref[0] > 0.5, _dot_nt(qf.astype(BF16), keys), 0.0)
                for lv in range(N_LEVELS):
                    d_lv = lvl_ref[lv * HG_CHUNK:(lv + 1) * HG_CHUNK, hs]
                    keys = jnp.concatenate([(kk * d_lv).astype(BF16), no_keys], axis=0)
                    pair = _dot_nt((qf * d_lv).astype(BF16), keys)
                    scores = jnp.where(lmask_ref[lv + 1] > 0.5, pair, scores)
                sc_ref[ci * HG_HEADS + hh] = scores.astype(BF16)
            return carry

        lax.fori_loop(0, n_chunks, chunk_scores, 0)

    upper_half = lax.broadcasted_iota(jnp.int32, (HG_CHUNK, HG_DIM), 0) >= HG_HALF

    def chunk_decays(ci, hs):
        a_q = dec_ref[ci, 0:HG_CHUNK, hs]
        a_e = dec_ref[ci, HG_CHUNK:2 * HG_CHUNK, hs]
        first_half = a_q[HG_HALF - 1:HG_HALF]
        second_half = a_q[HG_CHUNK - 1:HG_CHUNK]
        d_cum = a_q * jnp.where(upper_half, first_half, 1.0)
        d_end = a_e * jnp.where(upper_half, 1.0, second_half)
        return d_cum, d_end, first_half * second_half

    hg_upd = {}
    for ci in range(n_chunks):
        rows = slice(ci * HG_CHUNK, (ci + 1) * HG_CHUNK)
        for hh in range(HG_HEADS):
            hs = slice(hh * HG_DIM, (hh + 1) * HG_DIM)
            kk = act_ref[rows, OFF_KIN + hs.start:OFF_KIN + hs.stop]
            v = act_ref[rows, OFF_HI + hs.start:OFF_HI + hs.stop].astype(BF16)
            _, d_end, _ = chunk_decays(ci, hs)
            hg_upd[ci, hh] = _dot_tn(v, (kk * d_end).astype(BF16))

    for i in range(RET_WIDTH // PROJ_COLS):
        silu_unit(OFF_RG + i * PROJ_COLS)
    for i in range(HG_WIDTH // PROJ_COLS):
        silu_unit(OFF_HG + i * PROJ_COLS)

    for hh in range(HG_HEADS):
        hs = slice(hh * HG_DIM, (hh + 1) * HG_DIM)
        st = st_ref[hh]
        for ci in range(n_chunks):
            rows = slice(ci * HG_CHUNK, (ci + 1) * HG_CHUNK)
            qf = act_ref[rows, OFF_HQ + hs.start:OFF_HQ + hs.stop]
            v = act_ref[rows, OFF_HI + hs.start:OFF_HI + hs.stop].astype(BF16)
            d_cum, _, d_all = chunk_decays(ci, hs)
            o = (_dot(sc_ref[ci * HG_HEADS + hh], jnp.concatenate([v, v], axis=0))
                 + _dot_nt((qf * d_cum).astype(BF16), st.astype(BF16)))
            st = st * d_all + hg_upd[ci, hh]
            gate = act_ref[rows, OFF_HG + hs.start:OFF_HG + hs.stop]
            mix_ref[rows, RET_WIDTH + hs.start:RET_WIDTH + hs.stop] = _rms(o, hgw_ref[:, hs]) * gate
        st_ref[hh] = st
    for hh in range(RET_HEADS):
        vs = slice(hh * RET_V_DIM, (hh + 1) * RET_V_DIM)
        r_h = r_ref[hh]
        for rc in range(n_ret):
            rows = slice(rc * RET_CHUNK, (rc + 1) * RET_CHUNK)
            q = act_ref[rows, group_cols(OFF_RQ, hh)].astype(BF16)
            v_h = act_ref[rows, OFF_RV + vs.start:OFF_RV + vs.stop].astype(BF16)
            o = _dot(ret_scores[rc, hh], v_h) + _dot(q, r_h.astype(BF16)) * cross_ref[hh]
            r_h = chunk_decay[hh] * r_h + ret_upd[rc, hh]
            gate = act_ref[rows, OFF_RG + vs.start:OFF_RG + vs.stop]
            mix_ref[rows, vs] = _rms(o, retw_ref[:, vs]) * gate
        r_ref[hh] = r_h

    mixed = (_dot(mix_ref[:, RET_WIDTH:].astype(BF16), wout_ref[RET_WIDTH:, :])
             + _dot(mix_ref[:, :RET_WIDTH].astype(BF16), wout_ref[:RET_WIDTH, :]))
    o_ref[0] = x + g1 * mixed


def _const_spec(shape):
    zeros = (0,) * len(shape)
    return pl.BlockSpec(shape, lambda b, s: zeros, pipeline_mode=pl.Buffered(1))


def _layer_spec(layer, shape):
    idx = (layer,) + (0,) * len(shape)
    return pl.BlockSpec((None,) + tuple(shape), lambda b, s: idx, pipeline_mode=pl.Buffered(1))


def _mixer_call(layer, x, mod, norm_w, w_qk, w_in, cos, sin, ret_tabs, hg_tabs, lbs, ret_w, hg_w, w_out):
    inner, cross_b, state_b, chunk_decay = ret_tabs
    half_tab, direct_mask, level_tab, level_mask = hg_tabs
    n_chunks = SEQ_TILE // HG_CHUNK
    return pl.pallas_call(
        functools.partial(_mixer_kernel, layer, chunk_decay),
        grid=(BATCH, SEQ // SEQ_TILE),
        in_specs=[
            pl.BlockSpec((1, SEQ_TILE, D_MODEL), lambda b, s: (b, s, 0)),
            pl.BlockSpec((None, 1, N_MOD, D_MODEL), lambda b, s: (layer, b, 0, 0)),
            _layer_spec(layer, (1, D_MODEL)),
            _layer_spec(layer, (D_MODEL, 2 * RET_QK_WIDTH)),
            _layer_spec(layer, (D_MODEL, IN_WIDTH)),
            pl.BlockSpec((SEQ_TILE, LANES), lambda b, s: (s, 0)),
            pl.BlockSpec((SEQ_TILE, LANES), lambda b, s: (s, 0)),
            _const_spec((RET_HEADS, RET_CHUNK, RET_CHUNK)),
            _const_spec((RET_HEADS, RET_CHUNK, RET_V_DIM)),
            _const_spec((RET_HEADS, RET_CHUNK, LANES)),
            _const_spec((2 * HG_CHUNK, 2 * HG_CHUNK)),
            _const_spec((HG_CHUNK, 2 * HG_CHUNK)),
            _const_spec((N_LEVELS * HG_CHUNK, 2 * HG_CHUNK)),
            _const_spec((N_LEVELS + 1, HG_CHUNK, 2 * HG_CHUNK)),
            _const_spec((DEPTH, HG_WIDTH)),
            _layer_spec(layer, (1, RET_WIDTH)),
            _layer_spec(layer, (1, HG_WIDTH)),
            _layer_spec(layer, (D_MODEL, D_MODEL)),
        ],
        out_specs=pl.BlockSpec((1, SEQ_TILE, D_MODEL), lambda b, s: (b, s, 0)),
        out_shape=jax.ShapeDtypeStruct((BATCH, SEQ, D_MODEL), F32),
        scratch_shapes=[
            pltpu.VMEM((SEQ_TILE, ACT_WIDTH), F32),
            pltpu.VMEM((n_chunks, 3 * HG_CHUNK, HG_WIDTH), F32),
            pltpu.VMEM((N_LEVELS * HG_CHUNK, HG_WIDTH), F32),
            pltpu.VMEM((n_chunks * HG_HEADS, HG_CHUNK, 2 * HG_CHUNK), BF16),
            pltpu.VMEM((SEQ_TILE, D_MODEL), F32),
            pltpu.VMEM((RET_HEADS, LANES, RET_V_DIM), F32),
            pltpu.VMEM((HG_HEADS, HG_DIM, HG_DIM), F32),
        ],
        compiler_params=pltpu.CompilerParams(
            dimension_semantics=("arbitrary", "arbitrary"), vmem_limit_bytes=VMEM_LIMIT_BYTES),
        name=f"mixer_l{layer}",
    )(x, mod, norm_w, w_qk, w_in, cos, sin,
      jnp.asarray(inner), jnp.asarray(cross_b), jnp.asarray(state_b),
      jnp.asarray(half_tab, dtype=BF16), jnp.asarray(direct_mask),
      jnp.asarray(level_tab, dtype=BF16), jnp.asarray(level_mask),
      lbs, ret_w, hg_w, w_out)


def _ffn_kernel(final, x_ref, mod_ref, nw_ref, wg_ref, wu_ref, wd_ref, fw_ref, o_ref):
    mod = mod_ref[0]
    sh2, sc2, g2 = mod[3:4], mod[4:5], mod[5:6]
    for r in range(FFN_TILE // FFN_ROWS):
        rows = slice(r * FFN_ROWS, (r + 1) * FFN_ROWS)
        x = x_ref[0, rows, :]
        h = (_rms(x, nw_ref[...]) * (1.0 + sc2) + sh2).astype(BF16)
        acc = jnp.zeros((FFN_ROWS, D_MODEL), F32)
        for j in range(D_FF // FFN_CHUNK):
            cs = slice(j * FFN_CHUNK, (j + 1) * FFN_CHUNK)
            a = _silu(_dot(h, wg_ref[:, cs])) * _dot(h, wu_ref[:, cs])
            acc = acc + _dot(a.astype(BF16), wd_ref[cs, :])
        y = x + g2 * acc
        if final:
            y = _rms(y, fw_ref[...])
        o_ref[0, rows, :] = y


def _ffn_call(layer, x, mod, norm_w, w_gate, w_up, w_down, final_w):
    final = layer == DEPTH - 1
    return pl.pallas_call(
        functools.partial(_ffn_kernel, final),
        grid=(BATCH, SEQ // FFN_TILE),
        in_specs=[
            pl.BlockSpec((1, FFN_TILE, D_MODEL), lambda b, s: (b, s, 0)),
            pl.BlockSpec((None, 1, N_MOD, D_MODEL), lambda b, s: (layer, b, 0, 0)),
            _layer_spec(layer, (1, D_MODEL)),
            _layer_spec(layer, (D_MODEL, D_FF)),
            _layer_spec(layer, (D_MODEL, D_FF)),
            _layer_spec(layer, (D_FF, D_MODEL)),
            _const_spec((1, D_MODEL)),
        ],
        out_specs=pl.BlockSpec((1, FFN_TILE, D_MODEL), lambda b, s: (b, s, 0)),
        out_shape=jax.ShapeDtypeStruct((BATCH, SEQ, D_MODEL), F32),
        compiler_params=pltpu.CompilerParams(
            dimension_semantics=("arbitrary", "arbitrary"), vmem_limit_bytes=VMEM_LIMIT_BYTES),
        name="swiglu_final" if final else "swiglu",
    )(x, mod, norm_w, w_gate, w_up, w_down, final_w.reshape(1, D_MODEL))


@jax.jit
def kernel(x, c, w_ada, b_ada, norm_mix_w, w_in, ret_norm_w, hg_lower_bounds, hg_norm_w, w_out,
           norm_ffn_w, w_ffn_gate, w_ffn_up, w_ffn_down, final_norm_w):
    assert x.shape == (BATCH, SEQ, D_MODEL) and x.dtype == F32
    mod_all = _ada_call(c, w_ada, b_ada).reshape(DEPTH, BATCH, N_MOD, D_MODEL)
    cos, sin = _rotary_tables()
    ret_tabs = _ret_tables(RET_CHUNK)
    hg_tabs = _hg_tables()
    w_in_b = w_in.astype(BF16)
    w_qk = _deinterleave_qk(w_in_b[:, :, :2 * RET_QK_WIDTH])
    w_out_b = w_out.astype(BF16)
    w_gate_b, w_up_b, w_down_b = (w.astype(BF16) for w in (w_ffn_gate, w_ffn_up, w_ffn_down))
    norm_mix = norm_mix_w.reshape(DEPTH, 1, D_MODEL)
    norm_ffn = norm_ffn_w.reshape(DEPTH, 1, D_MODEL)
    ret_w = ret_norm_w.reshape(DEPTH, 1, RET_WIDTH)
    hg_w = hg_norm_w.reshape(DEPTH, 1, HG_WIDTH)
    for layer in range(DEPTH):
        x = _mixer_call(layer, x, mod_all, norm_mix, w_qk, w_in_b, cos, sin, ret_tabs, hg_tabs,
                        hg_lower_bounds, ret_w, hg_w, w_out_b)
        x = _ffn_call(layer, x, mod_all, norm_ffn, w_gate_b, w_up_b, w_down_b, final_norm_w)
    return x
```

```python
import functools

import numpy as np
import jax
import jax.numpy as jnp
from jax import lax
from jax.experimental import pallas as pl
from jax.experimental.pallas import tpu as pltpu

D_MODEL = 1024
BATCH = 8
SEQ = 2048
DEPTH = 2
RET_WIDTH = 512
HG_WIDTH = 512
RET_HEADS = 4
RET_V_DIM = 128
RET_QK_DIM = 64
RET_QK_WIDTH = 256
HG_HEADS = 4
HG_DIM = 128
D_FF = 2816
ROPE_BASE = 10000.0
EPS = 1e-6
N_MOD = 6
IN_WIDTH = 3584

OFF_RQ, OFF_RK, OFF_RV, OFF_RG, OFF_HQ, OFF_HF, OFF_HI, OFF_HG = (
    0, 256, 512, 1024, 1536, 2048, 2560, 3072)
OFF_KIN = IN_WIDTH
ACT_WIDTH = IN_WIDTH + HG_WIDTH

LANES = 128
PROJ_COLS = 256
SEQ_TILE = 512
RET_CHUNK = 256
HG_CHUNK = 64
FFN_TILE = 1024
FFN_ROWS = 512
FFN_CHUNK = 256
ADA_TILE = 1536
VMEM_LIMIT_BYTES = 56 * 1024 * 1024

F32 = jnp.float32
BF16 = jnp.bfloat16


def _hg_levels():
    out, m = [], HG_CHUNK // 2
    while m >= 1:
        out.append(m)
        m //= 2
    return out


HG_LEVELS = _hg_levels()
N_LEVELS = len(HG_LEVELS)
HG_HALF = HG_CHUNK // 2
HG_DIRECT_LIMIT = 60.0


def _hg_tables():
    c = HG_CHUNK
    r = np.arange(c)
    u = r[None, :]
    half_start = (r // HG_HALF) * HG_HALF
    half_end = half_start + HG_HALF - 1
    m_q = (u >= half_start[:, None]) & (u <= r[:, None])
    m_e = (u > r[:, None]) & (u <= half_end[:, None])
    half_tab = np.concatenate([m_q, m_e], axis=0).astype(np.float32)
    upper = r >= HG_HALF
    across = upper[:, None] & (~upper)[None, :]
    inside = (upper[:, None] == upper[None, :]) & (r[None, :] <= r[:, None])
    direct_mask = np.stack([across, inside]).astype(np.float32)

    mats, masks = [], [np.eye(c, dtype=bool)]
    for m in HG_LEVELS:
        blk, pos = r // (2 * m), r % (2 * m)
        mid = blk * 2 * m + m
        up = pos >= m
        mats.append(np.where(up[:, None], (u >= mid[:, None]) & (u <= r[:, None]),
                             (u > r[:, None]) & (u < mid[:, None])))
        masks.append((blk[:, None] == blk[None, :]) & up[:, None] & (~up)[None, :])
    level_tab = np.concatenate(mats, axis=0).astype(np.float32)
    level_mask = np.stack(masks).astype(np.float32)
    dup = lambda t: np.concatenate([t, t], axis=1)
    return dup(half_tab), direct_mask, dup(level_tab), level_mask


def _ret_tables(chunk):
    h = np.arange(RET_HEADS, dtype=np.float32)
    log_gamma = np.log(np.float32(1.0) - np.power(np.float32(2.0), np.float32(-5.0) - h)).astype(np.float32)
    idx = np.arange(chunk, dtype=np.float32)
    rel = idx[:, None] - idx[None, :]
    inner = np.exp(np.where(rel[None] >= 0, log_gamma[:, None, None] * rel[None], -np.inf)).astype(np.float32)
    cross = np.exp(log_gamma[:, None] * (idx[None, :] + np.float32(1.0))).astype(np.float32)
    state = np.exp(log_gamma[:, None] * (np.float32(chunk) - np.float32(1.0) - idx[None, :])).astype(np.float32)
    chunk_decay = np.exp(log_gamma * np.float32(chunk)).astype(np.float32)
    cross_b = np.broadcast_to(cross[:, :, None], (RET_HEADS, chunk, RET_V_DIM)).copy()
    state_b = np.broadcast_to(state[:, :, None], (RET_HEADS, chunk, LANES)).copy()
    return inner, cross_b, state_b, [float(v) for v in chunk_decay]


RET_HEADS_PER_GROUP = LANES // RET_QK_DIM


def _rotary_tables():
    half = RET_QK_DIM // 2
    inv = ROPE_BASE ** (-jnp.linspace(0.0, 1.0, half, dtype=F32))
    pos = jnp.arange(SEQ, dtype=F32)
    theta = pos[:, None] * inv[None, :]
    sin = jnp.tile(jnp.sin(theta), (1, RET_HEADS_PER_GROUP))
    return jnp.tile(jnp.cos(theta), (1, 2 * RET_HEADS_PER_GROUP)), jnp.concatenate([-sin, sin], axis=1)


def _deinterleave_qk(w_qk):
    half = RET_QK_DIM // 2
    lead = w_qk.shape[:-1]
    w = w_qk.reshape(lead + (2, RET_HEADS // RET_HEADS_PER_GROUP, RET_HEADS_PER_GROUP, half, 2))
    w = jnp.moveaxis(w, -1, -3)
    return w.reshape(lead + (2 * RET_QK_WIDTH,))


def _rms(x, w):
    return x * lax.rsqrt(jnp.mean(x * x, axis=-1, keepdims=True) + EPS) * w


def _silu(x):
    return x / (1.0 + jnp.exp(-x))


def _dot(a, b):
    return jnp.dot(a, b, preferred_element_type=F32)


def _dot_nt(a, b):
    return lax.dot_general(a, b, (((1,), (1,)), ((), ())), preferred_element_type=F32)


def _dot_tn(a, b):
    return lax.dot_general(a, b, (((0,), (0,)), ((), ())), preferred_element_type=F32)


def _ada_kernel(c_ref, w_ref, b_ref, o_ref):
    c_act = _silu(c_ref[...])
    o_ref[0] = _dot(c_act.astype(BF16), w_ref[0].astype(BF16)) + b_ref[0]


def _ada_call(c, w_ada, b_ada):
    n = N_MOD * D_MODEL
    return pl.pallas_call(
        _ada_kernel,
        grid=(DEPTH, n // ADA_TILE),
        in_specs=[
            pl.BlockSpec((BATCH, D_MODEL), lambda l, j: (0, 0)),
            pl.BlockSpec((1, D_MODEL, ADA_TILE), lambda l, j: (l, 0, j)),
            pl.BlockSpec((1, 1, ADA_TILE), lambda l, j: (l, 0, j)),
        ],
        out_specs=pl.BlockSpec((1, BATCH, ADA_TILE), lambda l, j: (l, 0, j)),
        out_shape=jax.ShapeDtypeStruct((DEPTH, BATCH, n), F32),
        compiler_params=pltpu.CompilerParams(
            dimension_semantics=("arbitrary", "arbitrary"), vmem_limit_bytes=VMEM_LIMIT_BYTES),
        name="adaln_mod",
    )(c, w_ada, b_ada.reshape(DEPTH, 1, n))


def _mixer_kernel(layer, chunk_decay,
                  x_ref, mod_ref, nw_ref, wqk_ref, win_ref, cos_ref, sin_ref, inner_ref, cross_ref, sdec_ref,
                  htab_ref, dmask_ref, ltab_ref, lmask_ref, lbs_ref, retw_ref, hgw_ref, wout_ref,
                  o_ref, act_ref, dec_ref, lvl_ref, sc_ref, mix_ref, r_ref, st_ref):
    @pl.when(pl.program_id(1) == 0)
    def _():
        r_ref[...] = jnp.zeros_like(r_ref)
        st_ref[...] = jnp.zeros_like(st_ref)

    x = x_ref[0]
    mod = mod_ref[0]
    sh1, sc1, g1 = mod[0:1], mod[1:2], mod[2:3]
    h = (_rms(x, nw_ref[...]) * (1.0 + sc1) + sh1).astype(BF16)

    def proj(off, width):
        w_ref = wqk_ref if off < OFF_RV else win_ref
        return _dot(h, w_ref[:, off:off + width])

    n_chunks = SEQ_TILE // HG_CHUNK
    cos, sin = cos_ref[...], sin_ref[...]
    k_scale = RET_QK_DIM ** -0.5

    if layer > 0:
        lbs = lbs_ref[...]
        ex = jnp.exp(lbs - jnp.max(lbs, axis=0, keepdims=True))
        probs = ex / jnp.sum(ex, axis=0, keepdims=True)
        lower_bound = jnp.sum(probs[1:layer + 1], axis=0, keepdims=True)

    def forget_unit(i):
        cols = slice(i * PROJ_COLS, (i + 1) * PROJ_COLS)
        hf = proj(OFF_HF + cols.start, PROJ_COLS)
        e = jnp.exp(-jnp.abs(hf))
        if layer == 0:
            log_f = jnp.minimum(hf, 0.0) - jnp.log(1.0 + e)
            k_in = jnp.where(hf >= 0.0, e, 1.0) / (1.0 + e)
        else:
            lb = lower_bound[:, cols]
            f = lb + (1.0 - lb) * (jnp.where(hf >= 0.0, 1.0, e) / (1.0 + e))
            log_f = jnp.log(f)
            k_in = 1.0 - f
        act_ref[:, OFF_HF + cols.start:OFF_HF + cols.stop] = log_f
        act_ref[:, OFF_KIN + cols.start:OFF_KIN + cols.stop] = k_in

    def plain_unit(off):
        act_ref[:, off:off + PROJ_COLS] = proj(off, PROJ_COLS)

    def silu_unit(off):
        act_ref[:, off:off + PROJ_COLS] = _silu(proj(off, PROJ_COLS))

    def rotary_unit(off, scale):
        p = proj(off, RET_QK_WIDTH)
        for grp in range(RET_QK_WIDTH // LANES):
            pg = p[:, grp * LANES:(grp + 1) * LANES]
            swapped = pltpu.roll(pg, LANES // 2, 1)
            act_ref[:, off + grp * LANES:off + (grp + 1) * LANES] = (pg * cos + swapped * sin) * scale

    for i in range(HG_WIDTH // PROJ_COLS):
        forget_unit(i)
        plain_unit(OFF_HI + i * PROJ_COLS)
    for i in range(HG_WIDTH // PROJ_COLS):
        silu_unit(OFF_HQ + i * PROJ_COLS)
        plain_unit(OFF_RV + i * PROJ_COLS)
    rotary_unit(OFF_RQ, 1.0)
    rotary_unit(OFF_RK, k_scale)

    def log_f_split(rows):
        g = act_ref[rows, OFF_HF:OFF_HF + HG_WIDTH]
        g_hi = g.astype(BF16)
        g_lo = (g - g_hi.astype(F32)).astype(BF16)
        return jnp.concatenate([g_hi, g_lo], axis=0)

    e_min = jnp.zeros((HG_CHUNK, HG_WIDTH), F32)
    for ci in range(n_chunks):
        expo = _dot(htab_ref[...], log_f_split(slice(ci * HG_CHUNK, (ci + 1) * HG_CHUNK)))
        e_q = expo[:HG_CHUNK]
        e_min = jnp.minimum(e_min, e_q)
        dec_ref[ci, :2 * HG_CHUNK] = jnp.exp(expo)
        dec_ref[ci, 2 * HG_CHUNK:] = jnp.exp(-e_q)
    direct_ok = jnp.min(e_min) >= -HG_DIRECT_LIMIT

    lane = lax.broadcasted_iota(jnp.int32, (1, LANES), 1)
    head_in_group = (lane % (LANES // 2)) // (RET_QK_DIM // 2)
    n_ret = SEQ_TILE // RET_CHUNK

    def group_cols(off, hh):
        start = off + (hh // RET_HEADS_PER_GROUP) * LANES
        return slice(start, start + LANES)

    ret_scores, ret_upd = {}, {}
    for rc in range(n_ret):
        rows = slice(rc * RET_CHUNK, (rc + 1) * RET_CHUNK)
        for hh in range(RET_HEADS):
            q = act_ref[rows, group_cols(OFF_RQ, hh)].astype(BF16)
            kf = act_ref[rows, group_cols(OFF_RK, hh)]
            k_h = jnp.where(head_in_group == hh % RET_HEADS_PER_GROUP, kf, 0.0)
            v_h = act_ref[rows, OFF_RV + hh * RET_V_DIM:OFF_RV + (hh + 1) * RET_V_DIM].astype(BF16)
            ret_scores[rc, hh] = (_dot_nt(q, k_h.astype(BF16)) * inner_ref[hh]).astype(BF16)
            ret_upd[rc, hh] = _dot_tn((k_h * sdec_ref[hh]).astype(BF16), v_h)

    @pl.when(direct_ok)
    def _():
        across, inside = dmask_ref[0] > 0.5, dmask_ref[1] > 0.5
        for ci in range(n_chunks):
            rows = slice(ci * HG_CHUNK, (ci + 1) * HG_CHUNK)
            for hh in range(HG_HEADS):
                hs = slice(hh * HG_DIM, (hh + 1) * HG_DIM)
                qf = act_ref[rows, OFF_HQ + hs.start:OFF_HQ + hs.stop]
                kk = act_ref[rows, OFF_KIN + hs.start:OFF_KIN + hs.stop]
                a_q = dec_ref[ci, 0:HG_CHUNK, hs]
                a_e = dec_ref[ci, HG_CHUNK:2 * HG_CHUNK, hs]
                a_qinv = dec_ref[ci, 2 * HG_CHUNK:, hs]
                q_t = (qf * a_q).astype(BF16)
                pair_across = _dot_nt(q_t, (kk * a_e).astype(BF16))
                pair_inside = _dot_nt(q_t, (kk * a_qinv).astype(BF16))
                scores = jnp.where(across, pair_across, jnp.where(inside, pair_inside, 0.0))
                sc_ref[ci * HG_HEADS + hh] = scores.astype(BF16)

    @pl.when(jnp.logical_not(direct_ok))
    def _():
        def chunk_scores(ci, carry):
            rows = pl.ds(pl.multiple_of(ci * HG_CHUNK, HG_CHUNK), HG_CHUNK)
            lvl_ref[...] = jnp.exp(_dot(ltab_ref[...], log_f_split(rows)))
            for hh in range(HG_HEADS):
                hs = slice(hh * HG_DIM, (hh + 1) * HG_DIM)
                qf = act_ref[rows, OFF_HQ + hs.start:OFF_HQ + hs.stop]
                kk = act_ref[rows, OFF_KIN + hs.start:OFF_KIN + hs.stop]
                scores = jnp.where(lmask_ref[0] > 0.5, _dot_nt(qf.astype(BF16), kk.astype(BF16)), 0.0)
                for lv in range(N_LEVELS):
                    d_lv = lvl_ref[lv * HG_CHUNK:(lv + 1) * HG_CHUNK, hs]
                    pair = _dot_nt((qf * d_lv).astype(BF16), (kk * d_lv).astype(BF16))
                    scores = jnp.where(lmask_ref[lv + 1] > 0.5, pair, scores)
                sc_ref[ci * HG_HEADS + hh] = scores.astype(BF16)
            return carry

        lax.fori_loop(0, n_chunks, chunk_scores, 0)

    upper_half = lax.broadcasted_iota(jnp.int32, (HG_CHUNK, HG_DIM), 0) >= HG_HALF

    def chunk_decays(ci, hs):
        a_q = dec_ref[ci, 0:HG_CHUNK, hs]
        a_e = dec_ref[ci, HG_CHUNK:2 * HG_CHUNK, hs]
        first_half = a_q[HG_HALF - 1:HG_HALF]
        second_half = a_q[HG_CHUNK - 1:HG_CHUNK]
        d_cum = a_q * jnp.where(upper_half, first_half, 1.0)
        d_end = a_e * jnp.where(upper_half, 1.0, second_half)
        return d_cum, d_end, first_half * second_half

    hg_upd = {}
    for ci in range(n_chunks):
        rows = slice(ci * HG_CHUNK, (ci + 1) * HG_CHUNK)
        for hh in range(HG_HEADS):
            hs = slice(hh * HG_DIM, (hh + 1) * HG_DIM)
            kk = act_ref[rows, OFF_KIN + hs.start:OFF_KIN + hs.stop]
            v = act_ref[rows, OFF_HI + hs.start:OFF_HI + hs.stop].astype(BF16)
            _, d_end, _ = chunk_decays(ci, hs)
            hg_upd[ci, hh] = _dot_tn(v, (kk * d_end).astype(BF16))

    for i in range(RET_WIDTH // PROJ_COLS):
        silu_unit(OFF_RG + i * PROJ_COLS)
    for i in range(HG_WIDTH // PROJ_COLS):
        silu_unit(OFF_HG + i * PROJ_COLS)

    for hh in range(HG_HEADS):
        hs = slice(hh * HG_DIM, (hh + 1) * HG_DIM)
        st = st_ref[hh]
        for ci in range(n_chunks):
            rows = slice(ci * HG_CHUNK, (ci + 1) * HG_CHUNK)
            qf = act_ref[rows, OFF_HQ + hs.start:OFF_HQ + hs.stop]
            v = act_ref[rows, OFF_HI + hs.start:OFF_HI + hs.stop].astype(BF16)
            d_cum, _, d_all = chunk_decays(ci, hs)
            o = (_dot(sc_ref[ci * HG_HEADS + hh], v)
                 + _dot_nt((qf * d_cum).astype(BF16), st.astype(BF16)))
            st = st * d_all + hg_upd[ci, hh]
            gate = act_ref[rows, OFF_HG + hs.start:OFF_HG + hs.stop]
            mix_ref[rows, RET_WIDTH + hs.start:RET_WIDTH + hs.stop] = _rms(o, hgw_ref[:, hs]) * gate
        st_ref[hh] = st
    for hh in range(RET_HEADS):
        vs = slice(hh * RET_V_DIM, (hh + 1) * RET_V_DIM)
        r_h = r_ref[hh]
        for rc in range(n_ret):
            rows = slice(rc * RET_CHUNK, (rc + 1) * RET_CHUNK)
            q = act_ref[rows, group_cols(OFF_RQ, hh)].astype(BF16)
            v_h = act_ref[rows, OFF_RV + vs.start:OFF_RV + vs.stop].astype(BF16)
            o = _dot(ret_scores[rc, hh], v_h) + _dot(q, r_h.astype(BF16)) * cross_ref[hh]
            r_h = chunk_decay[hh] * r_h + ret_upd[rc, hh]
            gate = act_ref[rows, OFF_RG + vs.start:OFF_RG + vs.stop]
            mix_ref[rows, vs] = _rms(o, retw_ref[:, vs]) * gate
        r_ref[hh] = r_h

    mixed = (_dot(mix_ref[:, RET_WIDTH:].astype(BF16), wout_ref[RET_WIDTH:, :])
             + _dot(mix_ref[:, :RET_WIDTH].astype(BF16), wout_ref[:RET_WIDTH, :]))
    o_ref[0] = x + g1 * mixed


def _const_spec(shape):
    zeros = (0,) * len(shape)
    return pl.BlockSpec(shape, lambda b, s: zeros, pipeline_mode=pl.Buffered(1))


def _layer_spec(layer, shape):
    idx = (layer,) + (0,) * len(shape)
    return pl.BlockSpec((None,) + tuple(shape), lambda b, s: idx, pipeline_mode=pl.Buffered(1))


def _mixer_call(layer, x, mod, norm_w, w_qk, w_in, cos, sin, ret_tabs, hg_tabs, lbs, ret_w, hg_w, w_out):
    inner, cross_b, state_b, chunk_decay = ret_tabs
    half_tab, direct_mask, level_tab, level_mask = hg_tabs
    n_chunks = SEQ_TILE // HG_CHUNK
    return pl.pallas_call(
        functools.partial(_mixer_kernel, layer, chunk_decay),
        grid=(BATCH, SEQ // SEQ_TILE),
        in_specs=[
            pl.BlockSpec((1, SEQ_TILE, D_MODEL), lambda b, s: (b, s, 0)),
            pl.BlockSpec((None, 1, N_MOD, D_MODEL), lambda b, s: (layer, b, 0, 0)),
            _layer_spec(layer, (1, D_MODEL)),
            _layer_spec(layer, (D_MODEL, 2 * RET_QK_WIDTH)),
            _layer_spec(layer, (D_MODEL, IN_WIDTH)),
            pl.BlockSpec((SEQ_TILE, LANES), lambda b, s: (s, 0)),
            pl.BlockSpec((SEQ_TILE, LANES), lambda b, s: (s, 0)),
            _const_spec((RET_HEADS, RET_CHUNK, RET_CHUNK)),
            _const_spec((RET_HEADS, RET_CHUNK, RET_V_DIM)),
            _const_spec((RET_HEADS, RET_CHUNK, LANES)),
            _const_spec((2 * HG_CHUNK, 2 * HG_CHUNK)),
            _const_spec((2, HG_CHUNK, HG_CHUNK)),
            _const_spec((N_LEVELS * HG_CHUNK, 2 * HG_CHUNK)),
            _const_spec((N_LEVELS + 1, HG_CHUNK, HG_CHUNK)),
            _const_spec((DEPTH, HG_WIDTH)),
            _layer_spec(layer, (1, RET_WIDTH)),
            _layer_spec(layer, (1, HG_WIDTH)),
            _layer_spec(layer, (D_MODEL, D_MODEL)),
        ],
        out_specs=pl.BlockSpec((1, SEQ_TILE, D_MODEL), lambda b, s: (b, s, 0)),
        out_shape=jax.ShapeDtypeStruct((BATCH, SEQ, D_MODEL), F32),
        scratch_shapes=[
            pltpu.VMEM((SEQ_TILE, ACT_WIDTH), F32),
            pltpu.VMEM((n_chunks, 3 * HG_CHUNK, HG_WIDTH), F32),
            pltpu.VMEM((N_LEVELS * HG_CHUNK, HG_WIDTH), F32),
            pltpu.VMEM((n_chunks * HG_HEADS, HG_CHUNK, HG_CHUNK), BF16),
            pltpu.VMEM((SEQ_TILE, D_MODEL), F32),
            pltpu.VMEM((RET_HEADS, LANES, RET_V_DIM), F32),
            pltpu.VMEM((HG_HEADS, HG_DIM, HG_DIM), F32),
        ],
        compiler_params=pltpu.CompilerParams(
            dimension_semantics=("arbitrary", "arbitrary"), vmem_limit_bytes=VMEM_LIMIT_BYTES),
        name=f"mixer_l{layer}",
    )(x, mod, norm_w, w_qk, w_in, cos, sin,
      jnp.asarray(inner), jnp.asarray(cross_b), jnp.asarray(state_b),
      jnp.asarray(half_tab, dtype=BF16), jnp.asarray(direct_mask),
      jnp.asarray(level_tab, dtype=BF16), jnp.asarray(level_mask),
      lbs, ret_w, hg_w, w_out)


def _ffn_kernel(final, x_ref, mod_ref, nw_ref, wg_ref, wu_ref, wd_ref, fw_ref, o_ref, a_ref):
    mod = mod_ref[0]
    sh2, sc2, g2 = mod[3:4], mod[4:5], mod[5:6]
    for r in range(FFN_TILE // FFN_ROWS):
        rows = slice(r * FFN_ROWS, (r + 1) * FFN_ROWS)
        x = x_ref[0, rows, :]
        h = (_rms(x, nw_ref[...]) * (1.0 + sc2) + sh2).astype(BF16)
        for j in range(D_FF // FFN_CHUNK):
            cs = slice(j * FFN_CHUNK, (j + 1) * FFN_CHUNK)
            a_ref[r, :, cs] = (_silu(_dot(h, wg_ref[:, cs])) * _dot(h, wu_ref[:, cs])).astype(BF16)
        y = x + g2 * _dot(a_ref[r], wd_ref[...])
        if final:
            y = _rms(y, fw_ref[...])
        o_ref[0, rows, :] = y


def _ffn_call(layer, x, mod, norm_w, w_gate, w_up, w_down, final_w):
    final = layer == DEPTH - 1
    return pl.pallas_call(
        functools.partial(_ffn_kernel, final),
        grid=(BATCH, SEQ // FFN_TILE),
        in_specs=[
            pl.BlockSpec((1, FFN_TILE, D_MODEL), lambda b, s: (b, s, 0)),
            pl.BlockSpec((None, 1, N_MOD, D_MODEL), lambda b, s: (layer, b, 0, 0)),
            _layer_spec(layer, (1, D_MODEL)),
            _layer_spec(layer, (D_MODEL, D_FF)),
            _layer_spec(layer, (D_MODEL, D_FF)),
            _layer_spec(layer, (D_FF, D_MODEL)),
            _const_spec((1, D_MODEL)),
        ],
        out_specs=pl.BlockSpec((1, FFN_TILE, D_MODEL), lambda b, s: (b, s, 0)),
        out_shape=jax.ShapeDtypeStruct((BATCH, SEQ, D_MODEL), F32),
        scratch_shapes=[pltpu.VMEM((FFN_TILE // FFN_ROWS, FFN_ROWS, D_FF), BF16)],
        compiler_params=pltpu.CompilerParams(
            dimension_semantics=("arbitrary", "arbitrary"), vmem_limit_bytes=VMEM_LIMIT_BYTES),
        name="swiglu_final" if final else "swiglu",
    )(x, mod, norm_w, w_gate, w_up, w_down, final_w.reshape(1, D_MODEL))


@jax.jit
def kernel(x, c, w_ada, b_ada, norm_mix_w, w_in, ret_norm_w, hg_lower_bounds, hg_norm_w, w_out,
           norm_ffn_w, w_ffn_gate, w_ffn_up, w_ffn_down, final_norm_w):
    assert x.shape == (BATCH, SEQ, D_MODEL) and x.dtype == F32
    mod_all = _ada_call(c, w_ada, b_ada).reshape(DEPTH, BATCH, N_MOD, D_MODEL)
    cos, sin = _rotary_tables()
    ret_tabs = _ret_tables(RET_CHUNK)
    hg_tabs = _hg_tables()
    w_in_b = w_in.astype(BF16)
    w_qk = _deinterleave_qk(w_in_b[:, :, :2 * RET_QK_WIDTH])
    w_out_b = w_out.astype(BF16)
    w_gate_b, w_up_b, w_down_b = (w.astype(BF16) for w in (w_ffn_gate, w_ffn_up, w_ffn_down))
    norm_mix = norm_mix_w.reshape(DEPTH, 1, D_MODEL)
    norm_ffn = norm_ffn_w.reshape(DEPTH, 1, D_MODEL)
    ret_w = ret_norm_w.reshape(DEPTH, 1, RET_WIDTH)
    hg_w = hg_norm_w.reshape(DEPTH, 1, HG_WIDTH)
    for layer in range(DEPTH):
        x = _mixer_call(layer, x, mod_all, norm_mix, w_qk, w_in_b, cos, sin, ret_tabs, hg_tabs,
                        hg_lower_bounds, ret_w, hg_w, w_out_b)
        x = _ffn_call(layer, x, mod_all, norm_ffn, w_gate_b, w_up_b, w_down_b, final_norm_w)
    return x
```

```python
import functools

import numpy as np
import jax
import jax.numpy as jnp
from jax import lax
from jax.experimental import pallas as pl
from jax.experimental.pallas import tpu as pltpu

D_MODEL = 1024
BATCH = 8
SEQ = 2048
DEPTH = 2
RET_WIDTH = 512
HG_WIDTH = 512
RET_HEADS = 4
RET_V_DIM = 128
RET_QK_DIM = 64
RET_QK_WIDTH = 256
HG_HEADS = 4
HG_DIM = 128
D_FF = 2816
ROPE_BASE = 10000.0
EPS = 1e-6
N_MOD = 6
IN_WIDTH = 3584

OFF_RQ, OFF_RK, OFF_RV, OFF_RG, OFF_HQ, OFF_HF, OFF_HI, OFF_HG = (
    0, 256, 512, 1024, 1536, 2048, 2560, 3072)
OFF_KIN = IN_WIDTH
ACT_WIDTH = IN_WIDTH + HG_WIDTH

LANES = 128
BF16_SUBLANES = 16
PROJ_COLS = 256
SEQ_TILE = 512
RET_CHUNK = 256
HG_CHUNK = 64
FFN_TILE = 1024
FFN_ROWS = 512
FFN_CHUNK = 256
ADA_TILE = 1536
VMEM_LIMIT_BYTES = 56 * 1024 * 1024

F32 = jnp.float32
BF16 = jnp.bfloat16


def _hg_levels():
    out, m = [], HG_CHUNK // 2
    while m >= 1:
        out.append(m)
        m //= 2
    return out


HG_LEVELS = _hg_levels()
N_LEVELS = len(HG_LEVELS)
HG_HALF = HG_CHUNK // 2
HG_DIRECT_LIMIT = 60.0


def _hg_tables():
    c = HG_CHUNK
    r = np.arange(c)
    u = r[None, :]
    half_start = (r // HG_HALF) * HG_HALF
    half_end = half_start + HG_HALF - 1
    m_q = (u >= half_start[:, None]) & (u <= r[:, None])
    m_e = (u > r[:, None]) & (u <= half_end[:, None])
    half_tab = np.concatenate([m_q, m_e], axis=0).astype(np.float32)
    upper = r >= HG_HALF
    across = upper[:, None] & (~upper)[None, :]
    inside = (upper[:, None] == upper[None, :]) & (r[None, :] <= r[:, None])
    direct_mask = np.stack([across, inside]).astype(np.float32)

    mats, masks = [], [np.eye(c, dtype=bool)]
    for m in HG_LEVELS:
        blk, pos = r // (2 * m), r % (2 * m)
        mid = blk * 2 * m + m
        up = pos >= m
        mats.append(np.where(up[:, None], (u >= mid[:, None]) & (u <= r[:, None]),
                             (u > r[:, None]) & (u < mid[:, None])))
        masks.append((blk[:, None] == blk[None, :]) & up[:, None] & (~up)[None, :])
    level_tab = np.concatenate(mats, axis=0).astype(np.float32)
    level_mask = np.stack(masks).astype(np.float32)
    dup = lambda t: np.concatenate([t, t], axis=1)
    return dup(half_tab), direct_mask, dup(level_tab), level_mask


def _ret_tables(chunk):
    h = np.arange(RET_HEADS, dtype=np.float32)
    log_gamma = np.log(np.float32(1.0) - np.power(np.float32(2.0), np.float32(-5.0) - h)).astype(np.float32)
    idx = np.arange(chunk, dtype=np.float32)
    rel = idx[:, None] - idx[None, :]
    inner = np.exp(np.where(rel[None] >= 0, log_gamma[:, None, None] * rel[None], -np.inf)).astype(np.float32)
    cross = np.exp(log_gamma[:, None] * (idx[None, :] + np.float32(1.0))).astype(np.float32)
    state = np.exp(log_gamma[:, None] * (np.float32(chunk) - np.float32(1.0) - idx[None, :])).astype(np.float32)
    chunk_decay = np.exp(log_gamma * np.float32(chunk)).astype(np.float32)
    cross_b = np.broadcast_to(cross[:, :, None], (RET_HEADS, chunk, RET_V_DIM)).copy()
    state_b = np.broadcast_to(state[:, :, None], (RET_HEADS, chunk, LANES)).copy()
    return inner, cross_b, state_b, [float(v) for v in chunk_decay]


RET_HEADS_PER_GROUP = LANES // RET_QK_DIM


def _rotary_tables():
    half = RET_QK_DIM // 2
    inv = ROPE_BASE ** (-jnp.linspace(0.0, 1.0, half, dtype=F32))
    pos = jnp.arange(SEQ, dtype=F32)
    theta = pos[:, None] * inv[None, :]
    sin = jnp.tile(jnp.sin(theta), (1, RET_HEADS_PER_GROUP))
    return jnp.tile(jnp.cos(theta), (1, 2 * RET_HEADS_PER_GROUP)), jnp.concatenate([-sin, sin], axis=1)


def _deinterleave_qk(w_qk):
    half = RET_QK_DIM // 2
    lead = w_qk.shape[:-1]
    w = w_qk.reshape(lead + (2, RET_HEADS // RET_HEADS_PER_GROUP, RET_HEADS_PER_GROUP, half, 2))
    w = jnp.moveaxis(w, -1, -3)
    return w.reshape(lead + (2 * RET_QK_WIDTH,))


def _rms(x, w):
    return x * lax.rsqrt(jnp.mean(x * x, axis=-1, keepdims=True) + EPS) * w


def _silu(x):
    return x / (1.0 + jnp.exp(-x))


def _dot(a, b):
    return jnp.dot(a, b, preferred_element_type=F32)


def _dot_nt(a, b):
    return lax.dot_general(a, b, (((1,), (1,)), ((), ())), preferred_element_type=F32)


def _dot_tn(a, b):
    return lax.dot_general(a, b, (((0,), (0,)), ((), ())), preferred_element_type=F32)


def _ada_kernel(c_ref, w_ref, b_ref, o_ref):
    c_act = _silu(c_ref[...])
    o_ref[0] = _dot(c_act.astype(BF16), w_ref[0].astype(BF16)) + b_ref[0]


def _ada_call(c, w_ada, b_ada):
    n = N_MOD * D_MODEL
    return pl.pallas_call(
        _ada_kernel,
        grid=(DEPTH, n // ADA_TILE),
        in_specs=[
            pl.BlockSpec((BATCH, D_MODEL), lambda l, j: (0, 0)),
            pl.BlockSpec((1, D_MODEL, ADA_TILE), lambda l, j: (l, 0, j)),
            pl.BlockSpec((1, 1, ADA_TILE), lambda l, j: (l, 0, j)),
        ],
        out_specs=pl.BlockSpec((1, BATCH, ADA_TILE), lambda l, j: (l, 0, j)),
        out_shape=jax.ShapeDtypeStruct((DEPTH, BATCH, n), F32),
        compiler_params=pltpu.CompilerParams(
            dimension_semantics=("arbitrary", "arbitrary"), vmem_limit_bytes=VMEM_LIMIT_BYTES),
        name="adaln_mod",
    )(c, w_ada, b_ada.reshape(DEPTH, 1, n))


def _mixer_kernel(layer, chunk_decay,
                  x_ref, mod_ref, nw_ref, wqk_ref, win_ref, cos_ref, sin_ref, inner_ref, cross_ref, sdec_ref,
                  htab_ref, dmask_ref, ltab_ref, lmask_ref, lbs_ref, retw_ref, hgw_ref, wout_ref,
                  wg32_ref, wu32_ref, wd32_ref,
                  o_ref, wg16_ref, wu16_ref, wd16_ref,
                  act_ref, dec_ref, lvl_ref, sc_ref, mix_ref, r_ref, st_ref):
    @pl.when(pl.program_id(1) == 0)
    def _():
        r_ref[...] = jnp.zeros_like(r_ref)
        st_ref[...] = jnp.zeros_like(st_ref)

    _cast_blocks((wg32_ref, wu32_ref, wd32_ref), (wg16_ref, wu16_ref, wd16_ref))

    x = x_ref[0]
    mod = mod_ref[0]
    sh1, sc1, g1 = mod[0:1], mod[1:2], mod[2:3]
    h = (_rms(x, nw_ref[...]) * (1.0 + sc1) + sh1).astype(BF16)

    def proj(off, width):
        w_ref = wqk_ref if off < OFF_RV else win_ref
        return _dot(h, w_ref[:, off:off + width])

    n_chunks = SEQ_TILE // HG_CHUNK
    cos, sin = cos_ref[...], sin_ref[...]
    k_scale = RET_QK_DIM ** -0.5

    if layer > 0:
        lbs = lbs_ref[...]
        ex = jnp.exp(lbs - jnp.max(lbs, axis=0, keepdims=True))
        probs = ex / jnp.sum(ex, axis=0, keepdims=True)
        lower_bound = jnp.sum(probs[1:layer + 1], axis=0, keepdims=True)

    def forget_unit(i):
        cols = slice(i * PROJ_COLS, (i + 1) * PROJ_COLS)
        hf = proj(OFF_HF + cols.start, PROJ_COLS)
        e = jnp.exp(-jnp.abs(hf))
        if layer == 0:
            log_f = jnp.minimum(hf, 0.0) - jnp.log(1.0 + e)
            k_in = jnp.where(hf >= 0.0, e, 1.0) / (1.0 + e)
        else:
            lb = lower_bound[:, cols]
            f = lb + (1.0 - lb) * (jnp.where(hf >= 0.0, 1.0, e) / (1.0 + e))
            log_f = jnp.log(f)
            k_in = 1.0 - f
        act_ref[:, OFF_HF + cols.start:OFF_HF + cols.stop] = log_f
        act_ref[:, OFF_KIN + cols.start:OFF_KIN + cols.stop] = k_in

    def plain_unit(off):
        act_ref[:, off:off + PROJ_COLS] = proj(off, PROJ_COLS)

    def silu_unit(off):
        act_ref[:, off:off + PROJ_COLS] = _silu(proj(off, PROJ_COLS))

    def rotary_unit(off, scale):
        p = proj(off, RET_QK_WIDTH)
        for grp in range(RET_QK_WIDTH // LANES):
            pg = p[:, grp * LANES:(grp + 1) * LANES]
            swapped = pltpu.roll(pg, LANES // 2, 1)
            act_ref[:, off + grp * LANES:off + (grp + 1) * LANES] = (pg * cos + swapped * sin) * scale

    for i in range(HG_WIDTH // PROJ_COLS):
        forget_unit(i)
        plain_unit(OFF_HI + i * PROJ_COLS)
    for i in range(HG_WIDTH // PROJ_COLS):
        silu_unit(OFF_HQ + i * PROJ_COLS)
        plain_unit(OFF_RV + i * PROJ_COLS)
    rotary_unit(OFF_RQ, 1.0)
    rotary_unit(OFF_RK, k_scale)

    def log_f_split(rows):
        g = act_ref[rows, OFF_HF:OFF_HF + HG_WIDTH]
        g_hi = g.astype(BF16)
        g_lo = (g - g_hi.astype(F32)).astype(BF16)
        return jnp.concatenate([g_hi, g_lo], axis=0)

    e_min = jnp.zeros((HG_CHUNK, HG_WIDTH), F32)
    for ci in range(n_chunks):
        expo = _dot(htab_ref[...], log_f_split(slice(ci * HG_CHUNK, (ci + 1) * HG_CHUNK)))
        e_q = expo[:HG_CHUNK]
        e_min = jnp.minimum(e_min, e_q)
        dec_ref[ci, :2 * HG_CHUNK] = jnp.exp(expo)
        dec_ref[ci, 2 * HG_CHUNK:] = jnp.exp(-e_q)
    direct_ok = jnp.min(e_min) >= -HG_DIRECT_LIMIT

    lane = lax.broadcasted_iota(jnp.int32, (1, LANES), 1)
    head_in_group = (lane % (LANES // 2)) // (RET_QK_DIM // 2)
    n_ret = SEQ_TILE // RET_CHUNK

    def group_cols(off, hh):
        start = off + (hh // RET_HEADS_PER_GROUP) * LANES
        return slice(start, start + LANES)

    ret_scores, ret_upd = {}, {}
    for rc in range(n_ret):
        rows = slice(rc * RET_CHUNK, (rc + 1) * RET_CHUNK)
        for hh in range(RET_HEADS):
            q = act_ref[rows, group_cols(OFF_RQ, hh)].astype(BF16)
            kf = act_ref[rows, group_cols(OFF_RK, hh)]
            k_h = jnp.where(head_in_group == hh % RET_HEADS_PER_GROUP, kf, 0.0)
            v_h = act_ref[rows, OFF_RV + hh * RET_V_DIM:OFF_RV + (hh + 1) * RET_V_DIM].astype(BF16)
            ret_scores[rc, hh] = (_dot_nt(q, k_h.astype(BF16)) * inner_ref[hh]).astype(BF16)
            ret_upd[rc, hh] = _dot_tn((k_h * sdec_ref[hh]).astype(BF16), v_h)

    @pl.when(direct_ok)
    def _():
        across, inside = dmask_ref[0] > 0.5, dmask_ref[1] > 0.5
        for ci in range(n_chunks):
            rows = slice(ci * HG_CHUNK, (ci + 1) * HG_CHUNK)
            for hh in range(HG_HEADS):
                hs = slice(hh * HG_DIM, (hh + 1) * HG_DIM)
                qf = act_ref[rows, OFF_HQ + hs.start:OFF_HQ + hs.stop]
                kk = act_ref[rows, OFF_KIN + hs.start:OFF_KIN + hs.stop]
                a_q = dec_ref[ci, 0:HG_CHUNK, hs]
                a_e = dec_ref[ci, HG_CHUNK:2 * HG_CHUNK, hs]
                a_qinv = dec_ref[ci, 2 * HG_CHUNK:, hs]
                q_t = (qf * a_q).astype(BF16)
                pair_across = _dot_nt(q_t, (kk * a_e).astype(BF16))
                pair_inside = _dot_nt(q_t, (kk * a_qinv).astype(BF16))
                scores = jnp.where(across, pair_across, jnp.where(inside, pair_inside, 0.0))
                sc_ref[ci * HG_HEADS + hh] = scores.astype(BF16)

    @pl.when(jnp.logical_not(direct_ok))
    def _():
        def chunk_scores(ci, carry):
            rows = pl.ds(pl.multiple_of(ci * HG_CHUNK, HG_CHUNK), HG_CHUNK)
            lvl_ref[...] = jnp.exp(_dot(ltab_ref[...], log_f_split(rows)))
            for hh in range(HG_HEADS):
                hs = slice(hh * HG_DIM, (hh + 1) * HG_DIM)
                qf = act_ref[rows, OFF_HQ + hs.start:OFF_HQ + hs.stop]
                kk = act_ref[rows, OFF_KIN + hs.start:OFF_KIN + hs.stop]
                scores = jnp.where(lmask_ref[0] > 0.5, _dot_nt(qf.astype(BF16), kk.astype(BF16)), 0.0)
                for lv in range(N_LEVELS):
                    d_lv = lvl_ref[lv * HG_CHUNK:(lv + 1) * HG_CHUNK, hs]
                    pair = _dot_nt((qf * d_lv).astype(BF16), (kk * d_lv).astype(BF16))
                    scores = jnp.where(lmask_ref[lv + 1] > 0.5, pair, scores)
                sc_ref[ci * HG_HEADS + hh] = scores.astype(BF16)
            return carry

        lax.fori_loop(0, n_chunks, chunk_scores, 0)

    upper_half = lax.broadcasted_iota(jnp.int32, (HG_CHUNK, HG_DIM), 0) >= HG_HALF

    def chunk_decays(ci, hs):
        a_q = dec_ref[ci, 0:HG_CHUNK, hs]
        a_e = dec_ref[ci, HG_CHUNK:2 * HG_CHUNK, hs]
        first_half = a_q[HG_HALF - 1:HG_HALF]
        second_half = a_q[HG_CHUNK - 1:HG_CHUNK]
        d_cum = a_q * jnp.where(upper_half, first_half, 1.0)
        d_end = a_e * jnp.where(upper_half, 1.0, second_half)
        return d_cum, d_end, first_half * second_half

    hg_upd = {}
    for ci in range(n_chunks):
        rows = slice(ci * HG_CHUNK, (ci + 1) * HG_CHUNK)
        for hh in range(HG_HEADS):
            hs = slice(hh * HG_DIM, (hh + 1) * HG_DIM)
            kk = act_ref[rows, OFF_KIN + hs.start:OFF_KIN + hs.stop]
            v = act_ref[rows, OFF_HI + hs.start:OFF_HI + hs.stop].astype(BF16)
            _, d_end, _ = chunk_decays(ci, hs)
            hg_upd[ci, hh] = _dot_tn(v, (kk * d_end).astype(BF16))

    for i in range(RET_WIDTH // PROJ_COLS):
        silu_unit(OFF_RG + i * PROJ_COLS)
    for i in range(HG_WIDTH // PROJ_COLS):
        silu_unit(OFF_HG + i * PROJ_COLS)

    for hh in range(HG_HEADS):
        hs = slice(hh * HG_DIM, (hh + 1) * HG_DIM)
        st = st_ref[hh]
        for ci in range(n_chunks):
            rows = slice(ci * HG_CHUNK, (ci + 1) * HG_CHUNK)
            qf = act_ref[rows, OFF_HQ + hs.start:OFF_HQ + hs.stop]
            v = act_ref[rows, OFF_HI + hs.start:OFF_HI + hs.stop].astype(BF16)
            d_cum, _, d_all = chunk_decays(ci, hs)
            o = (_dot(sc_ref[ci * HG_HEADS + hh], v)
                 + _dot_nt((qf * d_cum).astype(BF16), st.astype(BF16)))
            st = st * d_all + hg_upd[ci, hh]
            gate = act_ref[rows, OFF_HG + hs.start:OFF_HG + hs.stop]
            mix_ref[rows, RET_WIDTH + hs.start:RET_WIDTH + hs.stop] = _rms(o, hgw_ref[:, hs]) * gate
        st_ref[hh] = st
    for hh in range(RET_HEADS):
        vs = slice(hh * RET_V_DIM, (hh + 1) * RET_V_DIM)
        r_h = r_ref[hh]
        for rc in range(n_ret):
            rows = slice(rc * RET_CHUNK, (rc + 1) * RET_CHUNK)
            q = act_ref[rows, group_cols(OFF_RQ, hh)].astype(BF16)
            v_h = act_ref[rows, OFF_RV + vs.start:OFF_RV + vs.stop].astype(BF16)
            o = _dot(ret_scores[rc, hh], v_h) + _dot(q, r_h.astype(BF16)) * cross_ref[hh]
            r_h = chunk_decay[hh] * r_h + ret_upd[rc, hh]
            gate = act_ref[rows, OFF_RG + vs.start:OFF_RG + vs.stop]
            mix_ref[rows, vs] = _rms(o, retw_ref[:, vs]) * gate
        r_ref[hh] = r_h

    mixed = (_dot(mix_ref[:, RET_WIDTH:].astype(BF16), wout_ref[RET_WIDTH:, :])
             + _dot(mix_ref[:, :RET_WIDTH].astype(BF16), wout_ref[:RET_WIDTH, :]))
    o_ref[0] = x + g1 * mixed


def _cast_blocks(src_refs, dst_refs):
    for src_ref, dst_ref in zip(src_refs, dst_refs):
        dst_ref[...] = src_ref[...].astype(BF16)


def _const_spec(shape):
    zeros = (0,) * len(shape)
    return pl.BlockSpec(shape, lambda b, s: zeros, pipeline_mode=pl.Buffered(1))


def _layer_spec(layer, shape):
    idx = (layer,) + (0,) * len(shape)
    return pl.BlockSpec((None,) + tuple(shape), lambda b, s: idx, pipeline_mode=pl.Buffered(1))


def _cast_specs(layer, rows_total, cols, grid):
    n_steps = grid[0] * grid[1]
    rows = next(r for r in range(BF16_SUBLANES, rows_total + 1, BF16_SUBLANES)
                if rows_total % r == 0 and rows_total // r <= n_steps)
    last = rows_total // rows - 1

    def block(b, s):
        return jnp.minimum(b * grid[1] + s, last)

    return (pl.BlockSpec((None, rows, cols), lambda b, s: (layer, block(b, s), 0)),
            pl.BlockSpec((rows, cols), lambda b, s: (block(b, s), 0)))


def _mixer_call(layer, x, mod, norm_w, w_qk, w_in, cos, sin, ret_tabs, hg_tabs, lbs, ret_w, hg_w, w_out,
                ffn_weights):
    inner, cross_b, state_b, chunk_decay = ret_tabs
    half_tab, direct_mask, level_tab, level_mask = hg_tabs
    n_chunks = SEQ_TILE // HG_CHUNK
    grid = (BATCH, SEQ // SEQ_TILE)
    casts = [_cast_specs(layer, *w.shape[1:], grid) for w in ffn_weights]
    return pl.pallas_call(
        functools.partial(_mixer_kernel, layer, chunk_decay),
        grid=grid,
        in_specs=[
            pl.BlockSpec((1, SEQ_TILE, D_MODEL), lambda b, s: (b, s, 0)),
            pl.BlockSpec((None, 1, N_MOD, D_MODEL), lambda b, s: (layer, b, 0, 0)),
            _layer_spec(layer, (1, D_MODEL)),
            _layer_spec(layer, (D_MODEL, 2 * RET_QK_WIDTH)),
            _const_spec((D_MODEL, IN_WIDTH)),
            pl.BlockSpec((SEQ_TILE, LANES), lambda b, s: (s, 0)),
            pl.BlockSpec((SEQ_TILE, LANES), lambda b, s: (s, 0)),
            _const_spec((RET_HEADS, RET_CHUNK, RET_CHUNK)),
            _const_spec((RET_HEADS, RET_CHUNK, RET_V_DIM)),
            _const_spec((RET_HEADS, RET_CHUNK, LANES)),
            _const_spec((2 * HG_CHUNK, 2 * HG_CHUNK)),
            _const_spec((2, HG_CHUNK, HG_CHUNK)),
            _const_spec((N_LEVELS * HG_CHUNK, 2 * HG_CHUNK)),
            _const_spec((N_LEVELS + 1, HG_CHUNK, HG_CHUNK)),
            _const_spec((DEPTH, HG_WIDTH)),
            _layer_spec(layer, (1, RET_WIDTH)),
            _layer_spec(layer, (1, HG_WIDTH)),
            _const_spec((D_MODEL, D_MODEL)),
            *[src for src, _ in casts],
        ],
        out_specs=[pl.BlockSpec((1, SEQ_TILE, D_MODEL), lambda b, s: (b, s, 0)), *[dst for _, dst in casts]],
        out_shape=[jax.ShapeDtypeStruct((BATCH, SEQ, D_MODEL), F32),
                   *[jax.ShapeDtypeStruct(w.shape[1:], BF16) for w in ffn_weights]],
        scratch_shapes=[
            pltpu.VMEM((SEQ_TILE, ACT_WIDTH), F32),
            pltpu.VMEM((n_chunks, 3 * HG_CHUNK, HG_WIDTH), F32),
            pltpu.VMEM((N_LEVELS * HG_CHUNK, HG_WIDTH), F32),
            pltpu.VMEM((n_chunks * HG_HEADS, HG_CHUNK, HG_CHUNK), BF16),
            pltpu.VMEM((SEQ_TILE, D_MODEL), F32),
            pltpu.VMEM((RET_HEADS, LANES, RET_V_DIM), F32),
            pltpu.VMEM((HG_HEADS, HG_DIM, HG_DIM), F32),
        ],
        compiler_params=pltpu.CompilerParams(
            dimension_semantics=("arbitrary", "arbitrary"), vmem_limit_bytes=VMEM_LIMIT_BYTES),
        name=f"mixer_l{layer}",
    )(x, mod, norm_w, w_qk, w_in, cos, sin,
      jnp.asarray(inner), jnp.asarray(cross_b), jnp.asarray(state_b),
      jnp.asarray(half_tab, dtype=BF16), jnp.asarray(direct_mask),
      jnp.asarray(level_tab, dtype=BF16), jnp.asarray(level_mask),
      lbs, ret_w, hg_w, w_out, *ffn_weights)


def _ffn_kernel(final, x_ref, mod_ref, nw_ref, wg_ref, wu_ref, wd_ref, fw_ref, *refs):
    if final:
        o_ref, a_ref = refs
    else:
        win32_ref, wout32_ref, o_ref, win16_ref, wout16_ref, a_ref = refs
        _cast_blocks((win32_ref, wout32_ref), (win16_ref, wout16_ref))
    mod = mod_ref[0]
    sh2, sc2, g2 = mod[3:4], mod[4:5], mod[5:6]
    for r in range(FFN_TILE // FFN_ROWS):
        rows = slice(r * FFN_ROWS, (r + 1) * FFN_ROWS)
        x = x_ref[0, rows, :]
        h = (_rms(x, nw_ref[...]) * (1.0 + sc2) + sh2).astype(BF16)
        for j in range(D_FF // FFN_CHUNK):
            cs = slice(j * FFN_CHUNK, (j + 1) * FFN_CHUNK)
            a_ref[r, :, cs] = (_silu(_dot(h, wg_ref[:, cs])) * _dot(h, wu_ref[:, cs])).astype(BF16)
        y = x + g2 * _dot(a_ref[r], wd_ref[...])
        if final:
            y = _rms(y, fw_ref[...])
        o_ref[0, rows, :] = y


def _ffn_call(layer, x, mod, norm_w, w_gate, w_up, w_down, final_w, next_mixer_weights):
    final = layer == DEPTH - 1
    grid = (BATCH, SEQ // FFN_TILE)
    to_cast = () if final else next_mixer_weights
    casts = [_cast_specs(layer + 1, *w.shape[1:], grid) for w in to_cast]
    return pl.pallas_call(
        functools.partial(_ffn_kernel, final),
        grid=grid,
        in_specs=[
            pl.BlockSpec((1, FFN_TILE, D_MODEL), lambda b, s: (b, s, 0)),
            pl.BlockSpec((None, 1, N_MOD, D_MODEL), lambda b, s: (layer, b, 0, 0)),
            _layer_spec(layer, (1, D_MODEL)),
            _const_spec((D_MODEL, D_FF)),
            _const_spec((D_MODEL, D_FF)),
            _const_spec((D_FF, D_MODEL)),
            _const_spec((1, D_MODEL)),
            *[src for src, _ in casts],
        ],
        out_specs=[pl.BlockSpec((1, FFN_TILE, D_MODEL), lambda b, s: (b, s, 0)), *[dst for _, dst in casts]],
        out_shape=[jax.ShapeDtypeStruct((BATCH, SEQ, D_MODEL), F32),
                   *[jax.ShapeDtypeStruct(w.shape[1:], BF16) for w in to_cast]],
        scratch_shapes=[pltpu.VMEM((FFN_TILE // FFN_ROWS, FFN_ROWS, D_FF), BF16)],
        compiler_params=pltpu.CompilerParams(
            dimension_semantics=("arbitrary", "arbitrary"), vmem_limit_bytes=VMEM_LIMIT_BYTES),
        name="swiglu_final" if final else "swiglu",
    )(x, mod, norm_w, w_gate, w_up, w_down, final_w.reshape(1, D_MODEL), *to_cast)


@jax.jit
def kernel(x, c, w_ada, b_ada, norm_mix_w, w_in, ret_norm_w, hg_lower_bounds, hg_norm_w, w_out,
           norm_ffn_w, w_ffn_gate, w_ffn_up, w_ffn_down, final_norm_w):
    assert x.shape == (BATCH, SEQ, D_MODEL) and x.dtype == F32
    mod_all = _ada_call(c, w_ada, b_ada).reshape(DEPTH, BATCH, N_MOD, D_MODEL)
    cos, sin = _rotary_tables()
    ret_tabs = _ret_tables(RET_CHUNK)
    hg_tabs = _hg_tables()
    w_qk = _deinterleave_qk(w_in[:, :, :2 * RET_QK_WIDTH]).astype(BF16)
    w_in_b, w_out_b = w_in[0].astype(BF16), w_out[0].astype(BF16)
    norm_mix = norm_mix_w.reshape(DEPTH, 1, D_MODEL)
    norm_ffn = norm_ffn_w.reshape(DEPTH, 1, D_MODEL)
    ret_w = ret_norm_w.reshape(DEPTH, 1, RET_WIDTH)
    hg_w = hg_norm_w.reshape(DEPTH, 1, HG_WIDTH)
    for layer in range(DEPTH):
        x, w_gate_b, w_up_b, w_down_b = _mixer_call(
            layer, x, mod_all, norm_mix, w_qk, w_in_b, cos, sin, ret_tabs, hg_tabs,
            hg_lower_bounds, ret_w, hg_w, w_out_b, (w_ffn_gate, w_ffn_up, w_ffn_down))
        x, *next_weights = _ffn_call(layer, x, mod_all, norm_ffn, w_gate_b, w_up_b, w_down_b, final_norm_w,
                                     (w_in, w_out))
        if next_weights:
            w_in_b, w_out_b = next_weights
    return x
```

```python
import functools

import numpy as np
import jax
import jax.numpy as jnp
from jax import lax
from jax.experimental import pallas as pl
from jax.experimental.pallas import tpu as pltpu

D_MODEL = 1024
BATCH = 8
SEQ = 2048
DEPTH = 2
RET_WIDTH = 512
HG_WIDTH = 512
RET_HEADS = 4
RET_V_DIM = 128
RET_QK_DIM = 64
RET_QK_WIDTH = 256
HG_HEADS = 4
HG_DIM = 128
D_FF = 2816
ROPE_BASE = 10000.0
EPS = 1e-6
N_MOD = 6
IN_WIDTH = 3584

OFF_RQ, OFF_RK, OFF_RV, OFF_RG, OFF_HQ, OFF_HF, OFF_HI, OFF_HG = (
    0, 256, 512, 1024, 1536, 2048, 2560, 3072)
OFF_KIN = IN_WIDTH
ACT_WIDTH = IN_WIDTH + HG_WIDTH

LANES = 128
BF16_SUBLANES = 16
PROJ_COLS = 256
SEQ_TILE = 512
RET_CHUNK = 256
HG_CHUNK = 64
FFN_TILE = 1024
FFN_ROWS = 256
FFN_CHUNK = 256
ADA_TILE = 1536
VMEM_LIMIT_BYTES = 56 * 1024 * 1024

F32 = jnp.float32
BF16 = jnp.bfloat16


def _hg_levels():
    out, m = [], HG_CHUNK // 2
    while m >= 1:
        out.append(m)
        m //= 2
    return out


HG_LEVELS = _hg_levels()
N_LEVELS = len(HG_LEVELS)
HG_HALF = HG_CHUNK // 2
HG_DIRECT_LIMIT = 60.0


def _hg_tables():
    c = HG_CHUNK
    r = np.arange(c)
    u = r[None, :]
    half_start = (r // HG_HALF) * HG_HALF
    half_end = half_start + HG_HALF - 1
    m_q = (u >= half_start[:, None]) & (u <= r[:, None])
    m_e = (u > r[:, None]) & (u <= half_end[:, None])
    half_tab = np.concatenate([m_q, m_e], axis=0).astype(np.float32)
    upper = r >= HG_HALF
    across = upper[:, None] & (~upper)[None, :]
    inside = (upper[:, None] == upper[None, :]) & (r[None, :] <= r[:, None])
    direct_mask = np.stack([across, inside]).astype(np.float32)

    mats, masks = [], [np.eye(c, dtype=bool)]
    for m in HG_LEVELS:
        blk, pos = r // (2 * m), r % (2 * m)
        mid = blk * 2 * m + m
        up = pos >= m
        mats.append(np.where(up[:, None], (u >= mid[:, None]) & (u <= r[:, None]),
                             (u > r[:, None]) & (u < mid[:, None])))
        masks.append((blk[:, None] == blk[None, :]) & up[:, None] & (~up)[None, :])
    level_tab = np.concatenate(mats, axis=0).astype(np.float32)
    level_mask = np.stack(masks).astype(np.float32)
    dup = lambda t: np.concatenate([t, t], axis=1)
    return dup(half_tab), direct_mask, dup(level_tab), level_mask


def _ret_tables(chunk):
    h = np.arange(RET_HEADS, dtype=np.float32)
    log_gamma = np.log(np.float32(1.0) - np.power(np.float32(2.0), np.float32(-5.0) - h)).astype(np.float32)
    idx = np.arange(chunk, dtype=np.float32)
    rel = idx[:, None] - idx[None, :]
    inner = np.exp(np.where(rel[None] >= 0, log_gamma[:, None, None] * rel[None], -np.inf)).astype(np.float32)
    cross = np.exp(log_gamma[:, None] * (idx[None, :] + np.float32(1.0))).astype(np.float32)
    state = np.exp(log_gamma[:, None] * (np.float32(chunk) - np.float32(1.0) - idx[None, :])).astype(np.float32)
    chunk_decay = np.exp(log_gamma * np.float32(chunk)).astype(np.float32)
    cross_b = np.broadcast_to(cross[:, :, None], (RET_HEADS, chunk, RET_V_DIM)).copy()
    state_b = np.broadcast_to(state[:, :, None], (RET_HEADS, chunk, LANES)).copy()
    return inner, cross_b, state_b, [float(v) for v in chunk_decay]


RET_HEADS_PER_GROUP = LANES // RET_QK_DIM


def _rotary_tables():
    half = RET_QK_DIM // 2
    inv = np.power(np.float32(ROPE_BASE), -np.linspace(0.0, 1.0, half, dtype=np.float32)).astype(np.float32)
    pos = np.arange(SEQ, dtype=np.float32)
    theta = (pos[:, None] * inv[None, :]).astype(np.float32)
    sin = np.tile(np.sin(theta), (1, RET_HEADS_PER_GROUP))
    cos = np.tile(np.cos(theta), (1, 2 * RET_HEADS_PER_GROUP))
    return cos.astype(np.float32), np.concatenate([-sin, sin], axis=1).astype(np.float32)


def _deinterleave_qk(w_qk):
    half = RET_QK_DIM // 2
    lead = w_qk.shape[:-1]
    w = w_qk.reshape(lead + (2, RET_HEADS // RET_HEADS_PER_GROUP, RET_HEADS_PER_GROUP, half, 2))
    w = jnp.moveaxis(w, -1, -3)
    return w.reshape(lead + (2 * RET_QK_WIDTH,))


def _rms(x, w):
    return x * lax.rsqrt(jnp.mean(x * x, axis=-1, keepdims=True) + EPS) * w


def _silu(x):
    return x / (1.0 + jnp.exp(-x))


def _dot(a, b):
    return jnp.dot(a, b, preferred_element_type=F32)


def _dot_nt(a, b):
    return lax.dot_general(a, b, (((1,), (1,)), ((), ())), preferred_element_type=F32)


def _dot_tn(a, b):
    return lax.dot_general(a, b, (((0,), (0,)), ((), ())), preferred_element_type=F32)


def _ada_kernel(c_ref, w_ref, b_ref, win32_ref, wout32_ref, o_ref, win16_ref, wout16_ref):
    c_act = _silu(c_ref[...])
    bias = b_ref[pl.ds(pl.program_id(0), 1), :]
    o_ref[0] = _dot(c_act.astype(BF16), w_ref[0].astype(BF16)) + bias
    _cast_blocks((win32_ref, wout32_ref), (win16_ref, wout16_ref))


def _ada_call(c, w_ada, b_ada, first_mixer_weights):
    n = N_MOD * D_MODEL
    grid = (DEPTH, n // ADA_TILE)
    casts = [_cast_specs(0, *w.shape[1:], grid) for w in first_mixer_weights]
    return pl.pallas_call(
        _ada_kernel,
        grid=grid,
        in_specs=[
            pl.BlockSpec((BATCH, D_MODEL), lambda l, j: (0, 0)),
            pl.BlockSpec((1, D_MODEL, ADA_TILE), lambda l, j: (l, 0, j)),
            pl.BlockSpec((DEPTH, ADA_TILE), lambda l, j: (0, j)),
            *[src for src, _ in casts],
        ],
        out_specs=[pl.BlockSpec((1, BATCH, ADA_TILE), lambda l, j: (l, 0, j)), *[dst for _, dst in casts]],
        out_shape=[jax.ShapeDtypeStruct((DEPTH, BATCH, n), F32),
                   *[jax.ShapeDtypeStruct(w.shape[1:], BF16) for w in first_mixer_weights]],
        compiler_params=pltpu.CompilerParams(
            dimension_semantics=("arbitrary", "arbitrary"), vmem_limit_bytes=VMEM_LIMIT_BYTES),
        name="adaln_mod",
    )(c, w_ada, b_ada, *first_mixer_weights)


def _mixer_kernel(layer, chunk_decay,
                  x_ref, mod_ref, nw_ref, wqk_ref, win_ref, cos_ref, sin_ref, inner_ref, cross_ref, sdec_ref,
                  htab_ref, dmask_ref, ltab_ref, lmask_ref, lbs_ref, retw_ref, hgw_ref, wout_ref,
                  wg32_ref, wu32_ref, wd32_ref,
                  o_ref, wg16_ref, wu16_ref, wd16_ref,
                  act_ref, dec_ref, lvl_ref, sc_ref, mix_ref, r_ref, st_ref):
    @pl.when(pl.program_id(1) == 0)
    def _():
        r_ref[...] = jnp.zeros_like(r_ref)
        st_ref[...] = jnp.zeros_like(st_ref)

    _cast_blocks((wg32_ref, wu32_ref, wd32_ref), (wg16_ref, wu16_ref, wd16_ref))

    x = x_ref[0]
    sh1, sc1, g1 = _mod_vectors(mod_ref, 0)
    h = (_rms(x, nw_ref[layer:layer + 1, :]) * (1.0 + sc1) + sh1).astype(BF16)

    def proj(off, width):
        w_ref = wqk_ref if off < OFF_RV else win_ref
        return _dot(h, w_ref[:, off:off + width])

    n_chunks = SEQ_TILE // HG_CHUNK
    cos, sin = cos_ref[...], sin_ref[...]
    k_scale = RET_QK_DIM ** -0.5

    if layer > 0:
        lbs = lbs_ref[...]
        ex = jnp.exp(lbs - jnp.max(lbs, axis=0, keepdims=True))
        probs = ex / jnp.sum(ex, axis=0, keepdims=True)
        lower_bound = jnp.sum(probs[1:layer + 1], axis=0, keepdims=True)

    def forget_unit(i):
        cols = slice(i * PROJ_COLS, (i + 1) * PROJ_COLS)
        hf = proj(OFF_HF + cols.start, PROJ_COLS)
        e = jnp.exp(-jnp.abs(hf))
        if layer == 0:
            log_f = jnp.minimum(hf, 0.0) - jnp.log(1.0 + e)
            k_in = jnp.where(hf >= 0.0, e, 1.0) / (1.0 + e)
        else:
            lb = lower_bound[:, cols]
            f = lb + (1.0 - lb) * (jnp.where(hf >= 0.0, 1.0, e) / (1.0 + e))
            log_f = jnp.log(f)
            k_in = 1.0 - f
        act_ref[:, OFF_HF + cols.start:OFF_HF + cols.stop] = log_f
        act_ref[:, OFF_KIN + cols.start:OFF_KIN + cols.stop] = k_in

    def plain_unit(off):
        act_ref[:, off:off + PROJ_COLS] = proj(off, PROJ_COLS)

    def silu_unit(off):
        act_ref[:, off:off + PROJ_COLS] = _silu(proj(off, PROJ_COLS))

    def rotary_unit(off, scale):
        p = proj(off, RET_QK_WIDTH)
        for grp in range(RET_QK_WIDTH // LANES):
            pg = p[:, grp * LANES:(grp + 1) * LANES]
            swapped = pltpu.roll(pg, LANES // 2, 1)
            act_ref[:, off + grp * LANES:off + (grp + 1) * LANES] = (pg * cos + swapped * sin) * scale

    for i in range(HG_WIDTH // PROJ_COLS):
        forget_unit(i)
        plain_unit(OFF_HI + i * PROJ_COLS)
    for i in range(HG_WIDTH // PROJ_COLS):
        silu_unit(OFF_HQ + i * PROJ_COLS)
        plain_unit(OFF_RV + i * PROJ_COLS)
    rotary_unit(OFF_RQ, 1.0)
    rotary_unit(OFF_RK, k_scale)

    def log_f_split(rows):
        g = act_ref[rows, OFF_HF:OFF_HF + HG_WIDTH]
        g_hi = g.astype(BF16)
        g_lo = (g - g_hi.astype(F32)).astype(BF16)
        return jnp.concatenate([g_hi, g_lo], axis=0)

    e_min = jnp.zeros((HG_CHUNK, HG_WIDTH), F32)
    for ci in range(n_chunks):
        expo = _dot(htab_ref[...], log_f_split(slice(ci * HG_CHUNK, (ci + 1) * HG_CHUNK)))
        e_q = expo[:HG_CHUNK]
        e_min = jnp.minimum(e_min, e_q)
        dec_ref[ci, :2 * HG_CHUNK] = jnp.exp(expo)
        dec_ref[ci, 2 * HG_CHUNK:] = jnp.exp(-e_q)
    direct_ok = jnp.min(e_min) >= -HG_DIRECT_LIMIT

    lane = lax.broadcasted_iota(jnp.int32, (1, LANES), 1)
    head_in_group = (lane % (LANES // 2)) // (RET_QK_DIM // 2)
    n_ret = SEQ_TILE // RET_CHUNK

    def group_cols(off, hh):
        start = off + (hh // RET_HEADS_PER_GROUP) * LANES
        return slice(start, start + LANES)

    ret_scores, ret_upd = {}, {}
    for rc in range(n_ret):
        rows = slice(rc * RET_CHUNK, (rc + 1) * RET_CHUNK)
        for hh in range(RET_HEADS):
            q = act_ref[rows, group_cols(OFF_RQ, hh)].astype(BF16)
            kf = act_ref[rows, group_cols(OFF_RK, hh)]
            k_h = jnp.where(head_in_group == hh % RET_HEADS_PER_GROUP, kf, 0.0)
            v_h = act_ref[rows, OFF_RV + hh * RET_V_DIM:OFF_RV + (hh + 1) * RET_V_DIM].astype(BF16)
            ret_scores[rc, hh] = (_dot_nt(q, k_h.astype(BF16)) * inner_ref[hh]).astype(BF16)
            ret_upd[rc, hh] = _dot_tn((k_h * sdec_ref[hh]).astype(BF16), v_h)

    across, inside = dmask_ref[0] > 0.5, dmask_ref[1] > 0.5
    for ci in range(n_chunks):
        rows = slice(ci * HG_CHUNK, (ci + 1) * HG_CHUNK)
        for hh in range(HG_HEADS):
            hs = slice(hh * HG_DIM, (hh + 1) * HG_DIM)
            qf = act_ref[rows, OFF_HQ + hs.start:OFF_HQ + hs.stop]
            kk = act_ref[rows, OFF_KIN + hs.start:OFF_KIN + hs.stop]
            a_q = dec_ref[ci, 0:HG_CHUNK, hs]
            a_e = dec_ref[ci, HG_CHUNK:2 * HG_CHUNK, hs]
            a_qinv = dec_ref[ci, 2 * HG_CHUNK:, hs]
            q_t = (qf * a_q).astype(BF16)
            pair_across = _dot_nt(q_t, (kk * a_e).astype(BF16))
            pair_inside = _dot_nt(q_t, (kk * a_qinv).astype(BF16))
            scores = jnp.where(across, pair_across, jnp.where(inside, pair_inside, 0.0))
            sc_ref[ci * HG_HEADS + hh] = scores.astype(BF16)

    upper_half = lax.broadcasted_iota(jnp.int32, (HG_CHUNK, HG_DIM), 0) >= HG_HALF

    def chunk_decays(ci, hs):
        a_q = dec_ref[ci, 0:HG_CHUNK, hs]
        a_e = dec_ref[ci, HG_CHUNK:2 * HG_CHUNK, hs]
        first_half = a_q[HG_HALF - 1:HG_HALF]
        second_half = a_q[HG_CHUNK - 1:HG_CHUNK]
        d_cum = a_q * jnp.where(upper_half, first_half, 1.0)
        d_end = a_e * jnp.where(upper_half, 1.0, second_half)
        return d_cum, d_end, first_half * second_half

    hg_upd = {}
    for ci in range(n_chunks):
        rows = slice(ci * HG_CHUNK, (ci + 1) * HG_CHUNK)
        for hh in range(HG_HEADS):
            hs = slice(hh * HG_DIM, (hh + 1) * HG_DIM)
            kk = act_ref[rows, OFF_KIN + hs.start:OFF_KIN + hs.stop]
            v = act_ref[rows, OFF_HI + hs.start:OFF_HI + hs.stop].astype(BF16)
            _, d_end, _ = chunk_decays(ci, hs)
            hg_upd[ci, hh] = _dot_tn(v, (kk * d_end).astype(BF16))

    for i in range(RET_WIDTH // PROJ_COLS):
        silu_unit(OFF_RG + i * PROJ_COLS)
    for i in range(HG_WIDTH // PROJ_COLS):
        silu_unit(OFF_HG + i * PROJ_COLS)

    @pl.when(jnp.logical_not(direct_ok))
    def _():
        def chunk_scores(ci, carry):
            rows = pl.ds(pl.multiple_of(ci * HG_CHUNK, HG_CHUNK), HG_CHUNK)
            lvl_ref[...] = jnp.exp(_dot(ltab_ref[...], log_f_split(rows)))
            for hh in range(HG_HEADS):
                hs = slice(hh * HG_DIM, (hh + 1) * HG_DIM)
                qf = act_ref[rows, OFF_HQ + hs.start:OFF_HQ + hs.stop]
                kk = act_ref[rows, OFF_KIN + hs.start:OFF_KIN + hs.stop]
                scores = jnp.where(lmask_ref[0] > 0.5, _dot_nt(qf.astype(BF16), kk.astype(BF16)), 0.0)
                for lv in range(N_LEVELS):
                    d_lv = lvl_ref[lv * HG_CHUNK:(lv + 1) * HG_CHUNK, hs]
                    pair = _dot_nt((qf * d_lv).astype(BF16), (kk * d_lv).astype(BF16))
                    scores = jnp.where(lmask_ref[lv + 1] > 0.5, pair, scores)
                sc_ref[ci * HG_HEADS + hh] = scores.astype(BF16)
            return carry

        lax.fori_loop(0, n_chunks, chunk_scores, 0)

    for hh in range(HG_HEADS):
        hs = slice(hh * HG_DIM, (hh + 1) * HG_DIM)
        st = st_ref[hh]
        for ci in range(n_chunks):
            rows = slice(ci * HG_CHUNK, (ci + 1) * HG_CHUNK)
            qf = act_ref[rows, OFF_HQ + hs.start:OFF_HQ + hs.stop]
            v = act_ref[rows, OFF_HI + hs.start:OFF_HI + hs.stop].astype(BF16)
            d_cum, _, d_all = chunk_decays(ci, hs)
            o = (_dot(sc_ref[ci * HG_HEADS + hh], v)
                 + _dot_nt((qf * d_cum).astype(BF16), st.astype(BF16)))
            st = st * d_all + hg_upd[ci, hh]
            gate = act_ref[rows, OFF_HG + hs.start:OFF_HG + hs.stop]
            mix_ref[rows, RET_WIDTH + hs.start:RET_WIDTH + hs.stop] = _rms(o, hgw_ref[layer:layer + 1, hs]) * gate
        st_ref[hh] = st
    for hh in range(RET_HEADS):
        vs = slice(hh * RET_V_DIM, (hh + 1) * RET_V_DIM)
        r_h = r_ref[hh]
        for rc in range(n_ret):
            rows = slice(rc * RET_CHUNK, (rc + 1) * RET_CHUNK)
            q = act_ref[rows, group_cols(OFF_RQ, hh)].astype(BF16)
            v_h = act_ref[rows, OFF_RV + vs.start:OFF_RV + vs.stop].astype(BF16)
            o = _dot(ret_scores[rc, hh], v_h) + _dot(q, r_h.astype(BF16)) * cross_ref[hh]
            r_h = chunk_decay[hh] * r_h + ret_upd[rc, hh]
            gate = act_ref[rows, OFF_RG + vs.start:OFF_RG + vs.stop]
            mix_ref[rows, vs] = _rms(o, retw_ref[layer:layer + 1, vs]) * gate
        r_ref[hh] = r_h

    mixed = (_dot(mix_ref[:, RET_WIDTH:].astype(BF16), wout_ref[RET_WIDTH:, :])
             + _dot(mix_ref[:, :RET_WIDTH].astype(BF16), wout_ref[:RET_WIDTH, :]))
    o_ref[0] = x + g1 * mixed


def _mod_vectors(mod_ref, first):
    row = mod_ref[pl.ds(pl.program_id(0), 1), :]
    return tuple(row[:, (first + k) * D_MODEL:(first + k + 1) * D_MODEL] for k in range(3))


def _cast_blocks(src_refs, dst_refs):
    for src_ref, dst_ref in zip(src_refs, dst_refs):
        dst_ref[...] = src_ref[...].astype(BF16)


def _const_spec(shape):
    zeros = (0,) * len(shape)
    return pl.BlockSpec(shape, lambda b, s: zeros, pipeline_mode=pl.Buffered(1))


def _layer_spec(layer, shape):
    idx = (layer,) + (0,) * len(shape)
    return pl.BlockSpec((None,) + tuple(shape), lambda b, s: idx, pipeline_mode=pl.Buffered(1))


def _cast_specs(layer, rows_total, cols, grid):
    n_steps = grid[0] * grid[1]
    rows = next(r for r in range(BF16_SUBLANES, rows_total + 1, BF16_SUBLANES)
                if rows_total % r == 0 and rows_total // r <= n_steps)
    last = rows_total // rows - 1

    def block(b, s):
        return jnp.minimum(b * grid[1] + s, last)

    return (pl.BlockSpec((None, rows, cols), lambda b, s: (layer, block(b, s), 0)),
            pl.BlockSpec((rows, cols), lambda b, s: (block(b, s), 0)))


def _mixer_call(layer, x, mod, norm_w, w_qk, w_in, cos, sin, ret_tabs, hg_tabs, lbs, ret_w, hg_w, w_out,
                ffn_weights):
    inner, cross_b, state_b, chunk_decay = ret_tabs
    half_tab, direct_mask, level_tab, level_mask = hg_tabs
    n_chunks = SEQ_TILE // HG_CHUNK
    grid = (BATCH, SEQ // SEQ_TILE)
    casts = [_cast_specs(layer, *w.shape[1:], grid) for w in ffn_weights]
    return pl.pallas_call(
        functools.partial(_mixer_kernel, layer, chunk_decay),
        grid=grid,
        in_specs=[
            pl.BlockSpec((1, SEQ_TILE, D_MODEL), lambda b, s: (b, s, 0)),
            _layer_spec(layer, (BATCH, N_MOD * D_MODEL)),
            _const_spec((DEPTH, D_MODEL)),
            _layer_spec(layer, (D_MODEL, 2 * RET_QK_WIDTH)),
            _const_spec((D_MODEL, IN_WIDTH)),
            pl.BlockSpec((SEQ_TILE, LANES), lambda b, s: (s, 0)),
            pl.BlockSpec((SEQ_TILE, LANES), lambda b, s: (s, 0)),
            _const_spec((RET_HEADS, RET_CHUNK, RET_CHUNK)),
            _const_spec((RET_HEADS, RET_CHUNK, RET_V_DIM)),
            _const_spec((RET_HEADS, RET_CHUNK, LANES)),
            _const_spec((2 * HG_CHUNK, 2 * HG_CHUNK)),
            _const_spec((2, HG_CHUNK, HG_CHUNK)),
            _const_spec((N_LEVELS * HG_CHUNK, 2 * HG_CHUNK)),
            _const_spec((N_LEVELS + 1, HG_CHUNK, HG_CHUNK)),
            _const_spec((DEPTH, HG_WIDTH)),
            _const_spec((DEPTH, RET_WIDTH)),
            _const_spec((DEPTH, HG_WIDTH)),
            _const_spec((D_MODEL, D_MODEL)),
            *[src for src, _ in casts],
        ],
        out_specs=[pl.BlockSpec((1, SEQ_TILE, D_MODEL), lambda b, s: (b, s, 0)), *[dst for _, dst in casts]],
        out_shape=[jax.ShapeDtypeStruct((BATCH, SEQ, D_MODEL), F32),
                   *[jax.ShapeDtypeStruct(w.shape[1:], BF16) for w in ffn_weights]],
        scratch_shapes=[
            pltpu.VMEM((SEQ_TILE, ACT_WIDTH), F32),
            pltpu.VMEM((n_chunks, 3 * HG_CHUNK, HG_WIDTH), F32),
            pltpu.VMEM((N_LEVELS * HG_CHUNK, HG_WIDTH), F32),
            pltpu.VMEM((n_chunks * HG_HEADS, HG_CHUNK, HG_CHUNK), BF16),
            pltpu.VMEM((SEQ_TILE, D_MODEL), F32),
            pltpu.VMEM((RET_HEADS, LANES, RET_V_DIM), F32),
            pltpu.VMEM((HG_HEADS, HG_DIM, HG_DIM), F32),
        ],
        compiler_params=pltpu.CompilerParams(
            dimension_semantics=("arbitrary", "arbitrary"), vmem_limit_bytes=VMEM_LIMIT_BYTES),
        name=f"mixer_l{layer}",
    )(x, mod, norm_w, w_qk, w_in, cos, sin,
      jnp.asarray(inner), jnp.asarray(cross_b), jnp.asarray(state_b),
      jnp.asarray(half_tab, dtype=BF16), jnp.asarray(direct_mask),
      jnp.asarray(level_tab, dtype=BF16), jnp.asarray(level_mask),
      lbs, ret_w, hg_w, w_out, *ffn_weights)


def _ffn_kernel(layer, x_ref, mod_ref, nw_ref, wg_ref, wu_ref, wd_ref, fw_ref, *refs):
    final = layer == DEPTH - 1
    if final:
        o_ref, a_ref = refs
    else:
        win32_ref, wout32_ref, o_ref, win16_ref, wout16_ref, a_ref = refs
        _cast_blocks((win32_ref, wout32_ref), (win16_ref, wout16_ref))
    sh2, sc2, g2 = _mod_vectors(mod_ref, 3)
    for r in range(FFN_TILE // FFN_ROWS):
        rows = slice(r * FFN_ROWS, (r + 1) * FFN_ROWS)
        x = x_ref[0, rows, :]
        h = (_rms(x, nw_ref[layer:layer + 1, :]) * (1.0 + sc2) + sh2).astype(BF16)
        for j in range(D_FF // FFN_CHUNK):
            cs = slice(j * FFN_CHUNK, (j + 1) * FFN_CHUNK)
            a_ref[r, :, cs] = (_silu(_dot(h, wg_ref[:, cs])) * _dot(h, wu_ref[:, cs])).astype(BF16)
        y = x + g2 * _dot(a_ref[r], wd_ref[...])
        if final:
            y = _rms(y, fw_ref[...])
        o_ref[0, rows, :] = y


def _ffn_call(layer, x, mod, norm_w, w_gate, w_up, w_down, final_w, next_mixer_weights):
    final = layer == DEPTH - 1
    grid = (BATCH, SEQ // FFN_TILE)
    to_cast = () if final else next_mixer_weights
    casts = [_cast_specs(layer + 1, *w.shape[1:], grid) for w in to_cast]
    return pl.pallas_call(
        functools.partial(_ffn_kernel, layer),
        grid=grid,
        in_specs=[
            pl.BlockSpec((1, FFN_TILE, D_MODEL), lambda b, s: (b, s, 0)),
            _layer_spec(layer, (BATCH, N_MOD * D_MODEL)),
            _const_spec((DEPTH, D_MODEL)),
            _const_spec((D_MODEL, D_FF)),
            _const_spec((D_MODEL, D_FF)),
            _const_spec((D_FF, D_MODEL)),
            _const_spec((1, D_MODEL)),
            *[src for src, _ in casts],
        ],
        out_specs=[pl.BlockSpec((1, FFN_TILE, D_MODEL), lambda b, s: (b, s, 0)), *[dst for _, dst in casts]],
        out_shape=[jax.ShapeDtypeStruct((BATCH, SEQ, D_MODEL), F32),
                   *[jax.ShapeDtypeStruct(w.shape[1:], BF16) for w in to_cast]],
        scratch_shapes=[pltpu.VMEM((FFN_TILE // FFN_ROWS, FFN_ROWS, D_FF), BF16)],
        compiler_params=pltpu.CompilerParams(
            dimension_semantics=("arbitrary", "arbitrary"), vmem_limit_bytes=VMEM_LIMIT_BYTES),
        name="swiglu_final" if final else "swiglu",
    )(x, mod, norm_w, w_gate, w_up, w_down, final_w.reshape(1, D_MODEL), *to_cast)


@jax.jit
def kernel(x, c, w_ada, b_ada, norm_mix_w, w_in, ret_norm_w, hg_lower_bounds, hg_norm_w, w_out,
           norm_ffn_w, w_ffn_gate, w_ffn_up, w_ffn_down, final_norm_w):
    assert x.shape == (BATCH, SEQ, D_MODEL) and x.dtype == F32
    mod_all, w_in_b, w_out_b = _ada_call(c, w_ada, b_ada, (w_in, w_out))
    cos, sin = _rotary_tables()
    ret_tabs = _ret_tables(RET_CHUNK)
    hg_tabs = _hg_tables()
    w_qk = _deinterleave_qk(w_in[:, :, :2 * RET_QK_WIDTH]).astype(BF16)
    for layer in range(DEPTH):
        x, w_gate_b, w_up_b, w_down_b = _mixer_call(
            layer, x, mod_all, norm_mix_w, w_qk, w_in_b, cos, sin, ret_tabs, hg_tabs,
            hg_lower_bounds, ret_norm_w, hg_norm_w, w_out_b, (w_ffn_gate, w_ffn_up, w_ffn_down))
        x, *next_weights = _ffn_call(layer, x, mod_all, norm_ffn_w, w_gate_b, w_up_b, w_down_b, final_norm_w,
                                     (w_in, w_out))
        if next_weights:
            w_in_b, w_out_b = next_weights
    return x
```

```python
import functools

import numpy as np
import jax
import jax.numpy as jnp
from jax import lax
from jax.experimental import pallas as pl
from jax.experimental.pallas import tpu as pltpu

D_MODEL = 1024
BATCH = 8
SEQ = 2048
DEPTH = 2
RET_WIDTH = 512
HG_WIDTH = 512
RET_HEADS = 4
RET_V_DIM = 128
RET_QK_DIM = 64
RET_QK_WIDTH = 256
HG_HEADS = 4
HG_DIM = 128
D_FF = 2816
ROPE_BASE = 10000.0
EPS = 1e-6
N_MOD = 6
IN_WIDTH = 3584

OFF_RQ, OFF_RK, OFF_RV, OFF_RG, OFF_HQ, OFF_HF, OFF_HI, OFF_HG = (
    0, 256, 512, 1024, 1536, 2048, 2560, 3072)
OFF_KIN = IN_WIDTH
ACT_WIDTH = IN_WIDTH + HG_WIDTH

LANES = 128
BF16_SUBLANES = 16
PROJ_COLS = 256
SEQ_TILE = 512
RET_CHUNK = 256
HG_CHUNK = 64
FFN_TILE = 1024
FFN_ROWS = 256
FFN_CHUNK = 256
ADA_TILE = 1536
VMEM_LIMIT_BYTES = 56 * 1024 * 1024

F32 = jnp.float32
BF16 = jnp.bfloat16


def _hg_levels():
    out, m = [], HG_CHUNK // 2
    while m >= 1:
        out.append(m)
        m //= 2
    return out


HG_LEVELS = _hg_levels()
N_LEVELS = len(HG_LEVELS)
HG_HALF = HG_CHUNK // 2
HG_DIRECT_LIMIT = 60.0


def _hg_tables():
    c = HG_CHUNK
    r = np.arange(c)
    u = r[None, :]
    half_start = (r // HG_HALF) * HG_HALF
    half_end = half_start + HG_HALF - 1
    m_q = (u >= half_start[:, None]) & (u <= r[:, None])
    m_e = (u > r[:, None]) & (u <= half_end[:, None])
    half_tab = np.concatenate([m_q, m_e], axis=0).astype(np.float32)
    upper = r >= HG_HALF
    across = upper[:, None] & (~upper)[None, :]
    inside = (upper[:, None] == upper[None, :]) & (r[None, :] <= r[:, None])
    direct_mask = np.stack([across, inside]).astype(np.float32)

    mats, masks = [], [np.eye(c, dtype=bool)]
    for m in HG_LEVELS:
        blk, pos = r // (2 * m), r % (2 * m)
        mid = blk * 2 * m + m
        up = pos >= m
        mats.append(np.where(up[:, None], (u >= mid[:, None]) & (u <= r[:, None]),
                             (u > r[:, None]) & (u < mid[:, None])))
        masks.append((blk[:, None] == blk[None, :]) & up[:, None] & (~up)[None, :])
    level_tab = np.concatenate(mats, axis=0).astype(np.float32)
    level_mask = np.stack(masks).astype(np.float32)
    dup = lambda t: np.concatenate([t, t], axis=1)
    return dup(half_tab), direct_mask, dup(level_tab), level_mask


def _ret_tables(chunk):
    h = np.arange(RET_HEADS, dtype=np.float32)
    log_gamma = np.log(np.float32(1.0) - np.power(np.float32(2.0), np.float32(-5.0) - h)).astype(np.float32)
    idx = np.arange(chunk, dtype=np.float32)
    rel = idx[:, None] - idx[None, :]
    inner = np.exp(np.where(rel[None] >= 0, log_gamma[:, None, None] * rel[None], -np.inf)).astype(np.float32)
    cross = np.exp(log_gamma[:, None] * (idx[None, :] + np.float32(1.0))).astype(np.float32)
    state = np.exp(log_gamma[:, None] * (np.float32(chunk) - np.float32(1.0) - idx[None, :])).astype(np.float32)
    chunk_decay = np.exp(log_gamma * np.float32(chunk)).astype(np.float32)
    cross_b = np.broadcast_to(cross[:, :, None], (RET_HEADS, chunk, RET_V_DIM)).copy()
    state_b = np.broadcast_to(state[:, :, None], (RET_HEADS, chunk, LANES)).copy()
    return inner, cross_b, state_b, [float(v) for v in chunk_decay]


RET_HEADS_PER_GROUP = LANES // RET_QK_DIM


def _rotary_tables():
    half = RET_QK_DIM // 2
    inv = np.power(np.float32(ROPE_BASE), -np.linspace(0.0, 1.0, half, dtype=np.float32)).astype(np.float32)
    pos = np.arange(SEQ, dtype=np.float32)
    theta = (pos[:, None] * inv[None, :]).astype(np.float32)
    sin = np.tile(np.sin(theta), (1, RET_HEADS_PER_GROUP))
    cos = np.tile(np.cos(theta), (1, 2 * RET_HEADS_PER_GROUP))
    return cos.astype(np.float32), np.concatenate([-sin, sin], axis=1).astype(np.float32)


def _deinterleave_qk():
    half = RET_QK_DIM // 2
    src = np.arange(2 * RET_QK_WIDTH).reshape(2, RET_HEADS // RET_HEADS_PER_GROUP, RET_HEADS_PER_GROUP, half, 2)
    src = np.moveaxis(src, -1, -3).reshape(-1)
    perm = np.zeros((2 * RET_QK_WIDTH, 2 * RET_QK_WIDTH), np.float32)
    perm[src, np.arange(2 * RET_QK_WIDTH)] = 1.0
    return perm


def _rms(x, w):
    return x * lax.rsqrt(jnp.mean(x * x, axis=-1, keepdims=True) + EPS) * w


def _silu(x):
    return x / (1.0 + jnp.exp(-x))


def _dot(a, b):
    return jnp.dot(a, b, preferred_element_type=F32)


def _dot_nt(a, b):
    return lax.dot_general(a, b, (((1,), (1,)), ((), ())), preferred_element_type=F32)


def _dot_tn(a, b):
    return lax.dot_general(a, b, (((0,), (0,)), ((), ())), preferred_element_type=F32)


def _ada_kernel(c_ref, w_ref, b_ref, wqk32_ref, perm_ref, win32_ref, wout32_ref,
                o_ref, wqk16_ref, win16_ref, wout16_ref):
    c_act = _silu(c_ref[...])
    bias = b_ref[pl.ds(pl.program_id(0), 1), :]
    o_ref[0] = _dot(c_act.astype(BF16), w_ref[0].astype(BF16)) + bias
    wqk16_ref[...] = _dot(wqk32_ref[...].astype(BF16), perm_ref[...]).astype(BF16)
    _cast_blocks((win32_ref, wout32_ref), (win16_ref, wout16_ref))


def _ada_call(c, w_ada, b_ada, w_in, w_out):
    n = N_MOD * D_MODEL
    grid = (DEPTH, n // ADA_TILE)
    qk_width = 2 * RET_QK_WIDTH
    qk_rows = D_MODEL // grid[1]
    qk_spec = pl.BlockSpec((None, qk_rows, qk_width), lambda l, j: (l, j, 0))
    casts = [_cast_specs(0, *w.shape[1:], grid) for w in (w_in, w_out)]
    return pl.pallas_call(
        _ada_kernel,
        grid=grid,
        in_specs=[
            pl.BlockSpec((BATCH, D_MODEL), lambda l, j: (0, 0)),
            pl.BlockSpec((1, D_MODEL, ADA_TILE), lambda l, j: (l, 0, j)),
            pl.BlockSpec((DEPTH, ADA_TILE), lambda l, j: (0, j)),
            qk_spec,
            pl.BlockSpec((qk_width, qk_width), lambda l, j: (0, 0)),
            *[src for src, _ in casts],
        ],
        out_specs=[pl.BlockSpec((1, BATCH, ADA_TILE), lambda l, j: (l, 0, j)), qk_spec,
                   *[dst for _, dst in casts]],
        out_shape=[jax.ShapeDtypeStruct((DEPTH, BATCH, n), F32),
                   jax.ShapeDtypeStruct((DEPTH, D_MODEL, qk_width), BF16),
                   jax.ShapeDtypeStruct(w_in.shape[1:], BF16), jax.ShapeDtypeStruct(w_out.shape[1:], BF16)],
        compiler_params=pltpu.CompilerParams(
            dimension_semantics=("arbitrary", "arbitrary"), vmem_limit_bytes=VMEM_LIMIT_BYTES),
        name="adaln_mod",
    )(c, w_ada, b_ada, w_in, jnp.asarray(_deinterleave_qk(), dtype=BF16), w_in, w_out)


def _mixer_kernel(layer, chunk_decay,
                  x_ref, mod_ref, nw_ref, wqk_ref, win_ref, cos_ref, sin_ref, inner_ref, cross_ref, sdec_ref,
                  htab_ref, dmask_ref, ltab_ref, lmask_ref, lbs_ref, retw_ref, hgw_ref, wout_ref,
                  wg32_ref, wu32_ref, wd32_ref,
                  o_ref, wg16_ref, wu16_ref, wd16_ref,
                  act_ref, dec_ref, lvl_ref, sc_ref, mix_ref, r_ref, st_ref):
    @pl.when(pl.program_id(1) == 0)
    def _():
        r_ref[...] = jnp.zeros_like(r_ref)
        st_ref[...] = jnp.zeros_like(st_ref)

    _cast_blocks((wg32_ref, wu32_ref, wd32_ref), (wg16_ref, wu16_ref, wd16_ref))

    x = x_ref[0]
    sh1, sc1, g1 = _mod_vectors(mod_ref, 0)
    h = (_rms(x, nw_ref[layer:layer + 1, :]) * (1.0 + sc1) + sh1).astype(BF16)

    def proj(off, width):
        w_ref = wqk_ref if off < OFF_RV else win_ref
        return _dot(h, w_ref[:, off:off + width])

    n_chunks = SEQ_TILE // HG_CHUNK
    cos, sin = cos_ref[...], sin_ref[...]
    k_scale = RET_QK_DIM ** -0.5

    if layer > 0:
        lbs = lbs_ref[...]
        ex = jnp.exp(lbs - jnp.max(lbs, axis=0, keepdims=True))
        probs = ex / jnp.sum(ex, axis=0, keepdims=True)
        lower_bound = jnp.sum(probs[1:layer + 1], axis=0, keepdims=True)

    def forget_unit(i):
        cols = slice(i * PROJ_COLS, (i + 1) * PROJ_COLS)
        hf = proj(OFF_HF + cols.start, PROJ_COLS)
        e = jnp.exp(-jnp.abs(hf))
        if layer == 0:
            log_f = jnp.minimum(hf, 0.0) - jnp.log(1.0 + e)
            k_in = jnp.where(hf >= 0.0, e, 1.0) / (1.0 + e)
        else:
            lb = lower_bound[:, cols]
            f = lb + (1.0 - lb) * (jnp.where(hf >= 0.0, 1.0, e) / (1.0 + e))
            log_f = jnp.log(f)
            k_in = 1.0 - f
        act_ref[:, OFF_HF + cols.start:OFF_HF + cols.stop] = log_f
        act_ref[:, OFF_KIN + cols.start:OFF_KIN + cols.stop] = k_in

    def plain_unit(off):
        act_ref[:, off:off + PROJ_COLS] = proj(off, PROJ_COLS)

    def silu_unit(off):
        act_ref[:, off:off + PROJ_COLS] = _silu(proj(off, PROJ_COLS))

    def rotary_unit(off, scale):
        p = proj(off, RET_QK_WIDTH)
        for grp in range(RET_QK_WIDTH // LANES):
            pg = p[:, grp * LANES:(grp + 1) * LANES]
            swapped = pltpu.roll(pg, LANES // 2, 1)
            act_ref[:, off + grp * LANES:off + (grp + 1) * LANES] = (pg * cos + swapped * sin) * scale

    for i in range(HG_WIDTH // PROJ_COLS):
        forget_unit(i)
        plain_unit(OFF_HI + i * PROJ_COLS)
    for i in range(HG_WIDTH // PROJ_COLS):
        silu_unit(OFF_HQ + i * PROJ_COLS)
        plain_unit(OFF_RV + i * PROJ_COLS)
    rotary_unit(OFF_RQ, 1.0)
    rotary_unit(OFF_RK, k_scale)

    def log_f_split(rows):
        g = act_ref[rows, OFF_HF:OFF_HF + HG_WIDTH]
        g_hi = g.astype(BF16)
        g_lo = (g - g_hi.astype(F32)).astype(BF16)
        return jnp.concatenate([g_hi, g_lo], axis=0)

    e_min = jnp.zeros((HG_CHUNK, HG_WIDTH), F32)
    for ci in range(n_chunks):
        expo = _dot(htab_ref[...], log_f_split(slice(ci * HG_CHUNK, (ci + 1) * HG_CHUNK)))
        e_q = expo[:HG_CHUNK]
        e_min = jnp.minimum(e_min, e_q)
        dec_ref[ci, :2 * HG_CHUNK] = jnp.exp(expo)
        dec_ref[ci, 2 * HG_CHUNK:] = jnp.exp(-e_q)
    direct_ok = jnp.min(e_min) >= -HG_DIRECT_LIMIT

    lane = lax.broadcasted_iota(jnp.int32, (1, LANES), 1)
    head_in_group = (lane % (LANES // 2)) // (RET_QK_DIM // 2)
    n_ret = SEQ_TILE // RET_CHUNK

    def group_cols(off, hh):
        start = off + (hh // RET_HEADS_PER_GROUP) * LANES
        return slice(start, start + LANES)

    ret_scores, ret_upd = {}, {}
    for rc in range(n_ret):
        rows = slice(rc * RET_CHUNK, (rc + 1) * RET_CHUNK)
        for hh in range(RET_HEADS):
            q = act_ref[rows, group_cols(OFF_RQ, hh)].astype(BF16)
            kf = act_ref[rows, group_cols(OFF_RK, hh)]
            k_h = jnp.where(head_in_group == hh % RET_HEADS_PER_GROUP, kf, 0.0)
            v_h = act_ref[rows, OFF_RV + hh * RET_V_DIM:OFF_RV + (hh + 1) * RET_V_DIM].astype(BF16)
            ret_scores[rc, hh] = (_dot_nt(q, k_h.astype(BF16)) * inner_ref[hh]).astype(BF16)
            ret_upd[rc, hh] = _dot_tn((k_h * sdec_ref[hh]).astype(BF16), v_h)

    across, inside = dmask_ref[0] > 0.5, dmask_ref[1] > 0.5
    for ci in range(n_chunks):
        rows = slice(ci * HG_CHUNK, (ci + 1) * HG_CHUNK)
        for hh in range(HG_HEADS):
            hs = slice(hh * HG_DIM, (hh + 1) * HG_DIM)
            qf = act_ref[rows, OFF_HQ + hs.start:OFF_HQ + hs.stop]
            kk = act_ref[rows, OFF_KIN + hs.start:OFF_KIN + hs.stop]
            a_q = dec_ref[ci, 0:HG_CHUNK, hs]
            a_e = dec_ref[ci, HG_CHUNK:2 * HG_CHUNK, hs]
            a_qinv = dec_ref[ci, 2 * HG_CHUNK:, hs]
            q_t = (qf * a_q).astype(BF16)
            pair_across = _dot_nt(q_t, (kk * a_e).astype(BF16))
            pair_inside = _dot_nt(q_t, (kk * a_qinv).astype(BF16))
            scores = jnp.where(across, pair_across, jnp.where(inside, pair_inside, 0.0))
            sc_ref[ci * HG_HEADS + hh] = scores.astype(BF16)

    upper_half = lax.broadcasted_iota(jnp.int32, (HG_CHUNK, HG_DIM), 0) >= HG_HALF

    def chunk_decays(ci, hs):
        a_q = dec_ref[ci, 0:HG_CHUNK, hs]
        a_e = dec_ref[ci, HG_CHUNK:2 * HG_CHUNK, hs]
        first_half = a_q[HG_HALF - 1:HG_HALF]
        second_half = a_q[HG_CHUNK - 1:HG_CHUNK]
        d_cum = a_q * jnp.where(upper_half, first_half, 1.0)
        d_end = a_e * jnp.where(upper_half, 1.0, second_half)
        return d_cum, d_end, first_half * second_half

    hg_upd = {}
    for ci in range(n_chunks):
        rows = slice(ci * HG_CHUNK, (ci + 1) * HG_CHUNK)
        for hh in range(HG_HEADS):
            hs = slice(hh * HG_DIM, (hh + 1) * HG_DIM)
            kk = act_ref[rows, OFF_KIN + hs.start:OFF_KIN + hs.stop]
            v = act_ref[rows, OFF_HI + hs.start:OFF_HI + hs.stop].astype(BF16)
            _, d_end, _ = chunk_decays(ci, hs)
            hg_upd[ci, hh] = _dot_tn(v, (kk * d_end).astype(BF16))

    for i in range(RET_WIDTH // PROJ_COLS):
        silu_unit(OFF_RG + i * PROJ_COLS)
    for i in range(HG_WIDTH // PROJ_COLS):
        silu_unit(OFF_HG + i * PROJ_COLS)

    @pl.when(jnp.logical_not(direct_ok))
    def _():
        def chunk_scores(ci, carry):
            rows = pl.ds(pl.multiple_of(ci * HG_CHUNK, HG_CHUNK), HG_CHUNK)
            lvl_ref[...] = jnp.exp(_dot(ltab_ref[...], log_f_split(rows)))
            for hh in range(HG_HEADS):
                hs = slice(hh * HG_DIM, (hh + 1) * HG_DIM)
                qf = act_ref[rows, OFF_HQ + hs.start:OFF_HQ + hs.stop]
                kk = act_ref[rows, OFF_KIN + hs.start:OFF_KIN + hs.stop]
                scores = jnp.where(lmask_ref[0] > 0.5, _dot_nt(qf.astype(BF16), kk.astype(BF16)), 0.0)
                for lv in range(N_LEVELS):
                    d_lv = lvl_ref[lv * HG_CHUNK:(lv + 1) * HG_CHUNK, hs]
                    pair = _dot_nt((qf * d_lv).astype(BF16), (kk * d_lv).astype(BF16))
                    scores = jnp.where(lmask_ref[lv + 1] > 0.5, pair, scores)
                sc_ref[ci * HG_HEADS + hh] = scores.astype(BF16)
            return carry

        lax.fori_loop(0, n_chunks, chunk_scores, 0)

    for hh in range(HG_HEADS):
        hs = slice(hh * HG_DIM, (hh + 1) * HG_DIM)
        st = st_ref[hh]
        for ci in range(n_chunks):
            rows = slice(ci * HG_CHUNK, (ci + 1) * HG_CHUNK)
            qf = act_ref[rows, OFF_HQ + hs.start:OFF_HQ + hs.stop]
            v = act_ref[rows, OFF_HI + hs.start:OFF_HI + hs.stop].astype(BF16)
            d_cum, _, d_all = chunk_decays(ci, hs)
            o = (_dot(sc_ref[ci * HG_HEADS + hh], v)
                 + _dot_nt((qf * d_cum).astype(BF16), st.astype(BF16)))
            st = st * d_all + hg_upd[ci, hh]
            gate = act_ref[rows, OFF_HG + hs.start:OFF_HG + hs.stop]
            mix_ref[rows, RET_WIDTH + hs.start:RET_WIDTH + hs.stop] = _rms(o, hgw_ref[layer:layer + 1, hs]) * gate
        st_ref[hh] = st
    for hh in range(RET_HEADS):
        vs = slice(hh * RET_V_DIM, (hh + 1) * RET_V_DIM)
        r_h = r_ref[hh]
        for rc in range(n_ret):
            rows = slice(rc * RET_CHUNK, (rc + 1) * RET_CHUNK)
            q = act_ref[rows, group_cols(OFF_RQ, hh)].astype(BF16)
            v_h = act_ref[rows, OFF_RV + vs.start:OFF_RV + vs.stop].astype(BF16)
            o = _dot(ret_scores[rc, hh], v_h) + _dot(q, r_h.astype(BF16)) * cross_ref[hh]
            r_h = chunk_decay[hh] * r_h + ret_upd[rc, hh]
            gate = act_ref[rows, OFF_RG + vs.start:OFF_RG + vs.stop]
            mix_ref[rows, vs] = _rms(o, retw_ref[layer:layer + 1, vs]) * gate
        r_ref[hh] = r_h

    mixed = (_dot(mix_ref[:, RET_WIDTH:].astype(BF16), wout_ref[RET_WIDTH:, :])
             + _dot(mix_ref[:, :RET_WIDTH].astype(BF16), wout_ref[:RET_WIDTH, :]))
    o_ref[0] = x + g1 * mixed


def _mod_vectors(mod_ref, first):
    row = mod_ref[pl.ds(pl.program_id(0), 1), :]
    return tuple(row[:, (first + k) * D_MODEL:(first + k + 1) * D_MODEL] for k in range(3))


def _cast_blocks(src_refs, dst_refs):
    for src_ref, dst_ref in zip(src_refs, dst_refs):
        dst_ref[...] = src_ref[...].astype(BF16)


def _const_spec(shape):
    zeros = (0,) * len(shape)
    return pl.BlockSpec(shape, lambda b, s: zeros, pipeline_mode=pl.Buffered(1))


def _layer_spec(layer, shape):
    idx = (layer,) + (0,) * len(shape)
    return pl.BlockSpec((None,) + tuple(shape), lambda b, s: idx, pipeline_mode=pl.Buffered(1))


def _cast_specs(layer, rows_total, cols, grid):
    n_steps = grid[0] * grid[1]
    rows = next(r for r in range(BF16_SUBLANES, rows_total + 1, BF16_SUBLANES)
                if rows_total % r == 0 and rows_total // r <= n_steps)
    last = rows_total // rows - 1

    def block(b, s):
        return jnp.minimum(b * grid[1] + s, last)

    return (pl.BlockSpec((None, rows, cols), lambda b, s: (layer, block(b, s), 0)),
            pl.BlockSpec((rows, cols), lambda b, s: (block(b, s), 0)))


def _mixer_call(layer, x, mod, norm_w, w_qk, w_in, cos, sin, ret_tabs, hg_tabs, lbs, ret_w, hg_w, w_out,
                ffn_weights):
    inner, cross_b, state_b, chunk_decay = ret_tabs
    half_tab, direct_mask, level_tab, level_mask = hg_tabs
    n_chunks = SEQ_TILE // HG_CHUNK
    grid = (BATCH, SEQ // SEQ_TILE)
    casts = [_cast_specs(layer, *w.shape[1:], grid) for w in ffn_weights]
    return pl.pallas_call(
        functools.partial(_mixer_kernel, layer, chunk_decay),
        grid=grid,
        in_specs=[
            pl.BlockSpec((1, SEQ_TILE, D_MODEL), lambda b, s: (b, s, 0)),
            _layer_spec(layer, (BATCH, N_MOD * D_MODEL)),
            _const_spec((DEPTH, D_MODEL)),
            _layer_spec(layer, (D_MODEL, 2 * RET_QK_WIDTH)),
            _const_spec((D_MODEL, IN_WIDTH)),
            pl.BlockSpec((SEQ_TILE, LANES), lambda b, s: (s, 0)),
            pl.BlockSpec((SEQ_TILE, LANES), lambda b, s: (s, 0)),
            _const_spec((RET_HEADS, RET_CHUNK, RET_CHUNK)),
            _const_spec((RET_HEADS, RET_CHUNK, RET_V_DIM)),
            _const_spec((RET_HEADS, RET_CHUNK, LANES)),
            _const_spec((2 * HG_CHUNK, 2 * HG_CHUNK)),
            _const_spec((2, HG_CHUNK, HG_CHUNK)),
            _const_spec((N_LEVELS * HG_CHUNK, 2 * HG_CHUNK)),
            _const_spec((N_LEVELS + 1, HG_CHUNK, HG_CHUNK)),
            _const_spec((DEPTH, HG_WIDTH)),
            _const_spec((DEPTH, RET_WIDTH)),
            _const_spec((DEPTH, HG_WIDTH)),
            _const_spec((D_MODEL, D_MODEL)),
            *[src for src, _ in casts],
        ],
        out_specs=[pl.BlockSpec((1, SEQ_TILE, D_MODEL), lambda b, s: (b, s, 0)), *[dst for _, dst in casts]],
        out_shape=[jax.ShapeDtypeStruct((BATCH, SEQ, D_MODEL), F32),
                   *[jax.ShapeDtypeStruct(w.shape[1:], BF16) for w in ffn_weights]],
        scratch_shapes=[
            pltpu.VMEM((SEQ_TILE, ACT_WIDTH), F32),
            pltpu.VMEM((n_chunks, 3 * HG_CHUNK, HG_WIDTH), F32),
            pltpu.VMEM((N_LEVELS * HG_CHUNK, HG_WIDTH), F32),
            pltpu.VMEM((n_chunks * HG_HEADS, HG_CHUNK, HG_CHUNK), BF16),
            pltpu.VMEM((SEQ_TILE, D_MODEL), F32),
            pltpu.VMEM((RET_HEADS, LANES, RET_V_DIM), F32),
            pltpu.VMEM((HG_HEADS, HG_DIM, HG_DIM), F32),
        ],
        compiler_params=pltpu.CompilerParams(
            dimension_semantics=("arbitrary", "arbitrary"), vmem_limit_bytes=VMEM_LIMIT_BYTES),
        name=f"mixer_l{layer}",
    )(x, mod, norm_w, w_qk, w_in, cos, sin,
      jnp.asarray(inner), jnp.asarray(cross_b), jnp.asarray(state_b),
      jnp.asarray(half_tab, dtype=BF16), jnp.asarray(direct_mask),
      jnp.asarray(level_tab, dtype=BF16), jnp.asarray(level_mask),
      lbs, ret_w, hg_w, w_out, *ffn_weights)


def _ffn_kernel(layer, x_ref, mod_ref, nw_ref, wg_ref, wu_ref, wd_ref, fw_ref, *refs):
    final = layer == DEPTH - 1
    if final:
        o_ref, a_ref = refs
    else:
        win32_ref, wout32_ref, o_ref, win16_ref, wout16_ref, a_ref = refs
        _cast_blocks((win32_ref, wout32_ref), (win16_ref, wout16_ref))
    sh2, sc2, g2 = _mod_vectors(mod_ref, 3)
    for r in range(FFN_TILE // FFN_ROWS):
        rows = slice(r * FFN_ROWS, (r + 1) * FFN_ROWS)
        x = x_ref[0, rows, :]
        h = (_rms(x, nw_ref[layer:layer + 1, :]) * (1.0 + sc2) + sh2).astype(BF16)
        for j in range(D_FF // FFN_CHUNK):
            cs = slice(j * FFN_CHUNK, (j + 1) * FFN_CHUNK)
            a_ref[r, :, cs] = (_silu(_dot(h, wg_ref[:, cs])) * _dot(h, wu_ref[:, cs])).astype(BF16)
        y = x + g2 * _dot(a_ref[r], wd_ref[...])
        if final:
            y = _rms(y, fw_ref[...])
        o_ref[0, rows, :] = y


def _ffn_call(layer, x, mod, norm_w, w_gate, w_up, w_down, final_w, next_mixer_weights):
    final = layer == DEPTH - 1
    grid = (BATCH, SEQ // FFN_TILE)
    to_cast = () if final else next_mixer_weights
    casts = [_cast_specs(layer + 1, *w.shape[1:], grid) for w in to_cast]
    return pl.pallas_call(
        functools.partial(_ffn_kernel, layer),
        grid=grid,
        in_specs=[
            pl.BlockSpec((1, FFN_TILE, D_MODEL), lambda b, s: (b, s, 0)),
            _layer_spec(layer, (BATCH, N_MOD * D_MODEL)),
            _const_spec((DEPTH, D_MODEL)),
            _const_spec((D_MODEL, D_FF)),
            _const_spec((D_MODEL, D_FF)),
            _const_spec((D_FF, D_MODEL)),
            _const_spec((1, D_MODEL)),
            *[src for src, _ in casts],
        ],
        out_specs=[pl.BlockSpec((1, FFN_TILE, D_MODEL), lambda b, s: (b, s, 0)), *[dst for _, dst in casts]],
        out_shape=[jax.ShapeDtypeStruct((BATCH, SEQ, D_MODEL), F32),
                   *[jax.ShapeDtypeStruct(w.shape[1:], BF16) for w in to_cast]],
        scratch_shapes=[pltpu.VMEM((FFN_TILE // FFN_ROWS, FFN_ROWS, D_FF), BF16)],
        compiler_params=pltpu.CompilerParams(
            dimension_semantics=("arbitrary", "arbitrary"), vmem_limit_bytes=VMEM_LIMIT_BYTES),
        name="swiglu_final" if final else "swiglu",
    )(x, mod, norm_w, w_gate, w_up, w_down, final_w.reshape(1, D_MODEL), *to_cast)


@jax.jit
def kernel(x, c, w_ada, b_ada, norm_mix_w, w_in, ret_norm_w, hg_lower_bounds, hg_norm_w, w_out,
           norm_ffn_w, w_ffn_gate, w_ffn_up, w_ffn_down, final_norm_w):
    assert x.shape == (BATCH, SEQ, D_MODEL) and x.dtype == F32
    mod_all, w_qk, w_in_b, w_out_b = _ada_call(c, w_ada, b_ada, w_in, w_out)
    cos, sin = _rotary_tables()
    ret_tabs = _ret_tables(RET_CHUNK)
    hg_tabs = _hg_tables()
    for layer in range(DEPTH):
        x, w_gate_b, w_up_b, w_down_b = _mixer_call(
            layer, x, mod_all, norm_mix_w, w_qk, w_in_b, cos, sin, ret_tabs, hg_tabs,
            hg_lower_bounds, ret_norm_w, hg_norm_w, w_out_b, (w_ffn_gate, w_ffn_up, w_ffn_down))
        x, *next_weights = _ffn_call(layer, x, mod_all, norm_ffn_w, w_gate_b, w_up_b, w_down_b, final_norm_w,
                                     (w_in, w_out))
        if next_weights:
            w_in_b, w_out_b = next_weights
    return x
```

```python
import functools

import numpy as np
import jax
import jax.numpy as jnp
from jax import lax
from jax.experimental import pallas as pl
from jax.experimental.pallas import tpu as pltpu

D_MODEL = 1024
BATCH = 8
SEQ = 2048
DEPTH = 2
RET_WIDTH = 512
HG_WIDTH = 512
RET_HEADS = 4
RET_V_DIM = 128
RET_QK_DIM = 64
RET_QK_WIDTH = 256
HG_HEADS = 4
HG_DIM = 128
D_FF = 2816
ROPE_BASE = 10000.0
EPS = 1e-6
N_MOD = 6
IN_WIDTH = 3584

OFF_RQ, OFF_RK, OFF_RV, OFF_RG, OFF_HQ, OFF_HF, OFF_HI, OFF_HG = (
    0, 256, 512, 1024, 1536, 2048, 2560, 3072)
OFF_KIN = IN_WIDTH
ACT_WIDTH = IN_WIDTH + HG_WIDTH

LANES = 128
BF16_SUBLANES = 16
PROJ_COLS = 256
SEQ_TILE = 512
RET_CHUNK = 256
HG_CHUNK = 64
FFN_TILE = 1024
FFN_ROWS = 256
FFN_CHUNK = 256
ADA_TILE = 1536
V7X_VMEM_BYTES = 64 * 1024 * 1024
COMPILER_SCRATCH_BYTES = 12 * 1024 * 1024

F32 = jnp.float32
BF16 = jnp.bfloat16


def _hg_levels():
    out, m = [], HG_CHUNK // 2
    while m >= 1:
        out.append(m)
        m //= 2
    return out


HG_LEVELS = _hg_levels()
N_LEVELS = len(HG_LEVELS)
HG_HALF = HG_CHUNK // 2
HG_DIRECT_LIMIT = 60.0


def _hg_tables():
    c = HG_CHUNK
    r = np.arange(c)
    u = r[None, :]
    half_start = (r // HG_HALF) * HG_HALF
    half_end = half_start + HG_HALF - 1
    m_q = (u >= half_start[:, None]) & (u <= r[:, None])
    m_e = (u > r[:, None]) & (u <= half_end[:, None])
    half_tab = np.concatenate([m_q, m_e], axis=0).astype(np.float32)
    upper = r >= HG_HALF
    across = upper[:, None] & (~upper)[None, :]
    inside = (upper[:, None] == upper[None, :]) & (r[None, :] <= r[:, None])
    direct_mask = np.stack([across, inside]).astype(np.float32)

    mats, masks = [], [np.eye(c, dtype=bool)]
    for m in HG_LEVELS:
        blk, pos = r // (2 * m), r % (2 * m)
        mid = blk * 2 * m + m
        up = pos >= m
        mats.append(np.where(up[:, None], (u >= mid[:, None]) & (u <= r[:, None]),
                             (u > r[:, None]) & (u < mid[:, None])))
        masks.append((blk[:, None] == blk[None, :]) & up[:, None] & (~up)[None, :])
    level_tab = np.concatenate(mats, axis=0).astype(np.float32)
    level_mask = np.stack(masks).astype(np.float32)
    dup = lambda t: np.concatenate([t, t], axis=1)
    return dup(half_tab), direct_mask, dup(level_tab), level_mask


def _ret_tables(chunk):
    h = np.arange(RET_HEADS, dtype=np.float32)
    log_gamma = np.log(np.float32(1.0) - np.power(np.float32(2.0), np.float32(-5.0) - h)).astype(np.float32)
    idx = np.arange(chunk, dtype=np.float32)
    rel = idx[:, None] - idx[None, :]
    inner = np.exp(np.where(rel[None] >= 0, log_gamma[:, None, None] * rel[None], -np.inf)).astype(np.float32)
    cross = np.exp(log_gamma[:, None] * (idx[None, :] + np.float32(1.0))).astype(np.float32)
    state = np.exp(log_gamma[:, None] * (np.float32(chunk) - np.float32(1.0) - idx[None, :])).astype(np.float32)
    chunk_decay = np.exp(log_gamma * np.float32(chunk)).astype(np.float32)
    cross_b = np.broadcast_to(cross[:, :, None], (RET_HEADS, chunk, RET_V_DIM)).copy()
    state_b = np.broadcast_to(state[:, :, None], (RET_HEADS, chunk, LANES)).copy()
    return inner, cross_b, state_b, [float(v) for v in chunk_decay]


RET_HEADS_PER_GROUP = LANES // RET_QK_DIM


def _rotary_tables():
    half = RET_QK_DIM // 2
    inv = np.power(np.float32(ROPE_BASE), -np.linspace(0.0, 1.0, half, dtype=np.float32)).astype(np.float32)
    pos = np.arange(SEQ, dtype=np.float32)
    theta = (pos[:, None] * inv[None, :]).astype(np.float32)
    sin = np.tile(np.sin(theta), (1, RET_HEADS_PER_GROUP))
    cos = np.tile(np.cos(theta), (1, 2 * RET_HEADS_PER_GROUP))
    return cos.astype(np.float32), np.concatenate([-sin, sin], axis=1).astype(np.float32)


def _deinterleave_qk():
    half = RET_QK_DIM // 2
    src = np.arange(2 * RET_QK_WIDTH).reshape(2, RET_HEADS // RET_HEADS_PER_GROUP, RET_HEADS_PER_GROUP, half, 2)
    src = np.moveaxis(src, -1, -3).reshape(-1)
    perm = np.zeros((2 * RET_QK_WIDTH, 2 * RET_QK_WIDTH), np.float32)
    perm[src, np.arange(2 * RET_QK_WIDTH)] = 1.0
    return perm


def _rms(x, w):
    return x * lax.rsqrt(jnp.mean(x * x, axis=-1, keepdims=True) + EPS) * w


def _silu(x):
    return x / (1.0 + jnp.exp(-x))


def _dot(a, b):
    return jnp.dot(a, b, preferred_element_type=F32)


def _dot_nt(a, b):
    return lax.dot_general(a, b, (((1,), (1,)), ((), ())), preferred_element_type=F32)


def _dot_tn(a, b):
    return lax.dot_general(a, b, (((0,), (0,)), ((), ())), preferred_element_type=F32)


def _pallas_call(kernel_fn, name, grid, in_specs, operands, out_specs, out_shapes, scratch=()):
    def block_bytes(spec, dtype):
        n_buffers = 2 if spec.pipeline_mode is None else spec.pipeline_mode.buffer_count
        return n_buffers * int(np.prod([d for d in spec.block_shape if d is not None])) * jnp.dtype(dtype).itemsize

    declared = (sum(block_bytes(s, a.dtype) for s, a in zip(in_specs, operands))
                + sum(block_bytes(s, o.dtype) for s, o in zip(out_specs, out_shapes))
                + sum(int(np.prod(s.shape)) * jnp.dtype(s.dtype).itemsize for s in scratch))
    return pl.pallas_call(
        kernel_fn, grid=grid, in_specs=list(in_specs), out_specs=list(out_specs), out_shape=list(out_shapes),
        scratch_shapes=list(scratch),
        compiler_params=pltpu.CompilerParams(
            dimension_semantics=("arbitrary",) * len(grid),
            vmem_limit_bytes=min(declared + COMPILER_SCRATCH_BYTES, V7X_VMEM_BYTES)),
        name=name,
    )(*operands)


def _ada_kernel(c_ref, w_ref, b_ref, wqk32_ref, perm_ref, win32_ref, wout32_ref,
                o_ref, wqk16_ref, win16_ref, wout16_ref):
    c_act = _silu(c_ref[...])
    bias = b_ref[pl.ds(pl.program_id(0), 1), :]
    o_ref[0] = _dot(c_act.astype(BF16), w_ref[0].astype(BF16)) + bias
    wqk16_ref[...] = _dot(wqk32_ref[...].astype(BF16), perm_ref[...]).astype(BF16)
    _cast_blocks((win32_ref, wout32_ref), (win16_ref, wout16_ref))


def _ada_call(c, w_ada, b_ada, w_in, w_out):
    n = N_MOD * D_MODEL
    grid = (DEPTH, n // ADA_TILE)
    qk_width = 2 * RET_QK_WIDTH
    qk_rows = D_MODEL // grid[1]
    qk_spec = pl.BlockSpec((None, qk_rows, qk_width), lambda l, j: (l, j, 0))
    casts = [_cast_specs(0, *w.shape[1:], grid) for w in (w_in, w_out)]
    return _pallas_call(
        _ada_kernel, "adaln_mod", grid,
        in_specs=[
            pl.BlockSpec((BATCH, D_MODEL), lambda l, j: (0, 0)),
            pl.BlockSpec((1, D_MODEL, ADA_TILE), lambda l, j: (l, 0, j)),
            pl.BlockSpec((DEPTH, ADA_TILE), lambda l, j: (0, j)),
            qk_spec,
            pl.BlockSpec((qk_width, qk_width), lambda l, j: (0, 0)),
            *[src for src, _ in casts],
        ],
        operands=(c, w_ada, b_ada, w_in, jnp.asarray(_deinterleave_qk(), dtype=BF16), w_in, w_out),
        out_specs=[pl.BlockSpec((1, BATCH, ADA_TILE), lambda l, j: (l, 0, j)), qk_spec,
                   *[dst for _, dst in casts]],
        out_shapes=[jax.ShapeDtypeStruct((DEPTH, BATCH, n), F32),
                    jax.ShapeDtypeStruct((DEPTH, D_MODEL, qk_width), BF16),
                    jax.ShapeDtypeStruct(w_in.shape[1:], BF16), jax.ShapeDtypeStruct(w_out.shape[1:], BF16)])


def _mixer_kernel(layer, chunk_decay,
                  x_ref, mod_ref, nw_ref, wqk_ref, win_ref, cos_ref, sin_ref, inner_ref, cross_ref, sdec_ref,
                  htab_ref, dmask_ref, ltab_ref, lmask_ref, lbs_ref, retw_ref, hgw_ref, wout_ref,
                  wg32_ref, wu32_ref, wd32_ref,
                  o_ref, wg16_ref, wu16_ref, wd16_ref,
                  act_ref, dec_ref, lvl_ref, sc_ref, mix_ref, r_ref, st_ref):
    @pl.when(pl.program_id(1) == 0)
    def _():
        r_ref[...] = jnp.zeros_like(r_ref)
        st_ref[...] = jnp.zeros_like(st_ref)

    _cast_blocks((wg32_ref, wu32_ref, wd32_ref), (wg16_ref, wu16_ref, wd16_ref))

    x = x_ref[0]
    sh1, sc1, g1 = _mod_vectors(mod_ref, 0)
    h = (_rms(x, nw_ref[layer:layer + 1, :]) * (1.0 + sc1) + sh1).astype(BF16)

    def proj(off, width):
        w_ref = wqk_ref if off < OFF_RV else win_ref
        return _dot(h, w_ref[:, off:off + width])

    n_chunks = SEQ_TILE // HG_CHUNK
    cos, sin = cos_ref[...], sin_ref[...]
    k_scale = RET_QK_DIM ** -0.5

    if layer > 0:
        lbs = lbs_ref[...]
        ex = jnp.exp(lbs - jnp.max(lbs, axis=0, keepdims=True))
        probs = ex / jnp.sum(ex, axis=0, keepdims=True)
        lower_bound = jnp.sum(probs[1:layer + 1], axis=0, keepdims=True)

    def forget_unit(i):
        cols = slice(i * PROJ_COLS, (i + 1) * PROJ_COLS)
        hf = proj(OFF_HF + cols.start, PROJ_COLS)
        e = jnp.exp(-jnp.abs(hf))
        if layer == 0:
            log_f = jnp.minimum(hf, 0.0) - jnp.log(1.0 + e)
            k_in = jnp.where(hf >= 0.0, e, 1.0) / (1.0 + e)
        else:
            lb = lower_bound[:, cols]
            f = lb + (1.0 - lb) * (jnp.where(hf >= 0.0, 1.0, e) / (1.0 + e))
            log_f = jnp.log(f)
            k_in = 1.0 - f
        act_ref[:, OFF_HF + cols.start:OFF_HF + cols.stop] = log_f
        act_ref[:, OFF_KIN + cols.start:OFF_KIN + cols.stop] = k_in

    def plain_unit(off):
        act_ref[:, off:off + PROJ_COLS] = proj(off, PROJ_COLS)

    def silu_unit(off):
        act_ref[:, off:off + PROJ_COLS] = _silu(proj(off, PROJ_COLS))

    def rotary_unit(off, scale):
        p = proj(off, RET_QK_WIDTH)
        for grp in range(RET_QK_WIDTH // LANES):
            pg = p[:, grp * LANES:(grp + 1) * LANES]
            swapped = pltpu.roll(pg, LANES // 2, 1)
            act_ref[:, off + grp * LANES:off + (grp + 1) * LANES] = (pg * cos + swapped * sin) * scale

    for i in range(HG_WIDTH // PROJ_COLS):
        forget_unit(i)
        plain_unit(OFF_HI + i * PROJ_COLS)
    for i in range(HG_WIDTH // PROJ_COLS):
        silu_unit(OFF_HQ + i * PROJ_COLS)
        plain_unit(OFF_RV + i * PROJ_COLS)
    rotary_unit(OFF_RQ, 1.0)
    rotary_unit(OFF_RK, k_scale)

    def log_f_split(rows):
        g = act_ref[rows, OFF_HF:OFF_HF + HG_WIDTH]
        g_hi = g.astype(BF16)
        g_lo = (g - g_hi.astype(F32)).astype(BF16)
        return jnp.concatenate([g_hi, g_lo], axis=0)

    e_min = jnp.zeros((HG_CHUNK, HG_WIDTH), F32)
    for ci in range(n_chunks):
        expo = _dot(htab_ref[...], log_f_split(slice(ci * HG_CHUNK, (ci + 1) * HG_CHUNK)))
        e_q = expo[:HG_CHUNK]
        e_min = jnp.minimum(e_min, e_q)
        dec_ref[ci, :2 * HG_CHUNK] = jnp.exp(expo)
        dec_ref[ci, 2 * HG_CHUNK:] = jnp.exp(-e_q)
    direct_ok = jnp.min(e_min) >= -HG_DIRECT_LIMIT

    lane = lax.broadcasted_iota(jnp.int32, (1, LANES), 1)
    head_in_group = (lane % (LANES // 2)) // (RET_QK_DIM // 2)
    n_ret = SEQ_TILE // RET_CHUNK

    def group_cols(off, hh):
        start = off + (hh // RET_HEADS_PER_GROUP) * LANES
        return slice(start, start + LANES)

    ret_scores, ret_upd = {}, {}
    for rc in range(n_ret):
        rows = slice(rc * RET_CHUNK, (rc + 1) * RET_CHUNK)
        for hh in range(RET_HEADS):
            q = act_ref[rows, group_cols(OFF_RQ, hh)].astype(BF16)
            kf = act_ref[rows, group_cols(OFF_RK, hh)]
            k_h = jnp.where(head_in_group == hh % RET_HEADS_PER_GROUP, kf, 0.0)
            v_h = act_ref[rows, OFF_RV + hh * RET_V_DIM:OFF_RV + (hh + 1) * RET_V_DIM].astype(BF16)
            ret_scores[rc, hh] = (_dot_nt(q, k_h.astype(BF16)) * inner_ref[hh]).astype(BF16)
            ret_upd[rc, hh] = _dot_tn((k_h * sdec_ref[hh]).astype(BF16), v_h)

    across, inside = dmask_ref[0] > 0.5, dmask_ref[1] > 0.5
    for ci in range(n_chunks):
        rows = slice(ci * HG_CHUNK, (ci + 1) * HG_CHUNK)
        for hh in range(HG_HEADS):
            hs = slice(hh * HG_DIM, (hh + 1) * HG_DIM)
            qf = act_ref[rows, OFF_HQ + hs.start:OFF_HQ + hs.stop]
            kk = act_ref[rows, OFF_KIN + hs.start:OFF_KIN + hs.stop]
            a_q = dec_ref[ci, 0:HG_CHUNK, hs]
            a_e = dec_ref[ci, HG_CHUNK:2 * HG_CHUNK, hs]
            a_qinv = dec_ref[ci, 2 * HG_CHUNK:, hs]
            q_t = (qf * a_q).astype(BF16)
            pair_across = _dot_nt(q_t, (kk * a_e).astype(BF16))
            pair_inside = _dot_nt(q_t, (kk * a_qinv).astype(BF16))
            scores = jnp.where(across, pair_across, jnp.where(inside, pair_inside, 0.0))
            sc_ref[ci * HG_HEADS + hh] = scores.astype(BF16)

    upper_half = lax.broadcasted_iota(jnp.int32, (HG_CHUNK, HG_DIM), 0) >= HG_HALF

    def chunk_decays(ci, hs):
        a_q = dec_ref[ci, 0:HG_CHUNK, hs]
        a_e = dec_ref[ci, HG_CHUNK:2 * HG_CHUNK, hs]
        first_half = a_q[HG_HALF - 1:HG_HALF]
        second_half = a_q[HG_CHUNK - 1:HG_CHUNK]
        d_cum = a_q * jnp.where(upper_half, first_half, 1.0)
        d_end = a_e * jnp.where(upper_half, 1.0, second_half)
        return d_cum, d_end, first_half * second_half

    hg_upd = {}
    for ci in range(n_chunks):
        rows = slice(ci * HG_CHUNK, (ci + 1) * HG_CHUNK)
        for hh in range(HG_HEADS):
            hs = slice(hh * HG_DIM, (hh + 1) * HG_DIM)
            kk = act_ref[rows, OFF_KIN + hs.start:OFF_KIN + hs.stop]
            v = act_ref[rows, OFF_HI + hs.start:OFF_HI + hs.stop].astype(BF16)
            _, d_end, _ = chunk_decays(ci, hs)
            hg_upd[ci, hh] = _dot_tn(v, (kk * d_end).astype(BF16))

    for i in range(RET_WIDTH // PROJ_COLS):
        silu_unit(OFF_RG + i * PROJ_COLS)
    for i in range(HG_WIDTH // PROJ_COLS):
        silu_unit(OFF_HG + i * PROJ_COLS)

    @pl.when(jnp.logical_not(direct_ok))
    def _():
        def chunk_scores(ci, carry):
            rows = pl.ds(pl.multiple_of(ci * HG_CHUNK, HG_CHUNK), HG_CHUNK)
            lvl_ref[...] = jnp.exp(_dot(ltab_ref[...], log_f_split(rows)))
            for hh in range(HG_HEADS):
                hs = slice(hh * HG_DIM, (hh + 1) * HG_DIM)
                qf = act_ref[rows, OFF_HQ + hs.start:OFF_HQ + hs.stop]
                kk = act_ref[rows, OFF_KIN + hs.start:OFF_KIN + hs.stop]
                scores = jnp.where(lmask_ref[0] > 0.5, _dot_nt(qf.astype(BF16), kk.astype(BF16)), 0.0)
                for lv in range(N_LEVELS):
                    d_lv = lvl_ref[lv * HG_CHUNK:(lv + 1) * HG_CHUNK, hs]
                    pair = _dot_nt((qf * d_lv).astype(BF16), (kk * d_lv).astype(BF16))
                    scores = jnp.where(lmask_ref[lv + 1] > 0.5, pair, scores)
                sc_ref[ci * HG_HEADS + hh] = scores.astype(BF16)
            return carry

        lax.fori_loop(0, n_chunks, chunk_scores, 0)

    for hh in range(HG_HEADS):
        hs = slice(hh * HG_DIM, (hh + 1) * HG_DIM)
        st = st_ref[hh]
        for ci in range(n_chunks):
            rows = slice(ci * HG_CHUNK, (ci + 1) * HG_CHUNK)
            qf = act_ref[rows, OFF_HQ + hs.start:OFF_HQ + hs.stop]
            v = act_ref[rows, OFF_HI + hs.start:OFF_HI + hs.stop].astype(BF16)
            d_cum, _, d_all = chunk_decays(ci, hs)
            o = (_dot(sc_ref[ci * HG_HEADS + hh], v)
                 + _dot_nt((qf * d_cum).astype(BF16), st.astype(BF16)))
            st = st * d_all + hg_upd[ci, hh]
            gate = act_ref[rows, OFF_HG + hs.start:OFF_HG + hs.stop]
            mix_ref[rows, RET_WIDTH + hs.start:RET_WIDTH + hs.stop] = _rms(o, hgw_ref[layer:layer + 1, hs]) * gate
        st_ref[hh] = st
    for hh in range(RET_HEADS):
        vs = slice(hh * RET_V_DIM, (hh + 1) * RET_V_DIM)
        r_h = r_ref[hh]
        for rc in range(n_ret):
            rows = slice(rc * RET_CHUNK, (rc + 1) * RET_CHUNK)
            q = act_ref[rows, group_cols(OFF_RQ, hh)].astype(BF16)
            v_h = act_ref[rows, OFF_RV + vs.start:OFF_RV + vs.stop].astype(BF16)
            o = _dot(ret_scores[rc, hh], v_h) + _dot(q, r_h.astype(BF16)) * cross_ref[hh]
            r_h = chunk_decay[hh] * r_h + ret_upd[rc, hh]
            gate = act_ref[rows, OFF_RG + vs.start:OFF_RG + vs.stop]
            mix_ref[rows, vs] = _rms(o, retw_ref[layer:layer + 1, vs]) * gate
        r_ref[hh] = r_h

    mixed = (_dot(mix_ref[:, RET_WIDTH:].astype(BF16), wout_ref[RET_WIDTH:, :])
             + _dot(mix_ref[:, :RET_WIDTH].astype(BF16), wout_ref[:RET_WIDTH, :]))
    o_ref[0] = x + g1 * mixed


def _mod_vectors(mod_ref, first):
    row = mod_ref[pl.ds(pl.program_id(0), 1), :]
    return tuple(row[:, (first + k) * D_MODEL:(first + k + 1) * D_MODEL] for k in range(3))


def _cast_blocks(src_refs, dst_refs):
    for src_ref, dst_ref in zip(src_refs, dst_refs):
        dst_ref[...] = src_ref[...].astype(BF16)


def _const_spec(shape):
    zeros = (0,) * len(shape)
    return pl.BlockSpec(shape, lambda b, s: zeros, pipeline_mode=pl.Buffered(1))


def _layer_spec(layer, shape):
    idx = (layer,) + (0,) * len(shape)
    return pl.BlockSpec((None,) + tuple(shape), lambda b, s: idx, pipeline_mode=pl.Buffered(1))


def _cast_specs(layer, rows_total, cols, grid):
    n_steps = grid[0] * grid[1]
    rows = next(r for r in range(BF16_SUBLANES, rows_total + 1, BF16_SUBLANES)
                if rows_total % r == 0 and rows_total // r <= n_steps)
    last = rows_total // rows - 1

    def block(b, s):
        return jnp.minimum(b * grid[1] + s, last)

    return (pl.BlockSpec((None, rows, cols), lambda b, s: (layer, block(b, s), 0)),
            pl.BlockSpec((rows, cols), lambda b, s: (block(b, s), 0)))


def _mixer_call(layer, x, mod, norm_w, w_qk, w_in, cos, sin, ret_tabs, hg_tabs, lbs, ret_w, hg_w, w_out,
                ffn_weights):
    inner, cross_b, state_b, chunk_decay = ret_tabs
    half_tab, direct_mask, level_tab, level_mask = hg_tabs
    n_chunks = SEQ_TILE // HG_CHUNK
    grid = (BATCH, SEQ // SEQ_TILE)
    casts = [_cast_specs(layer, *w.shape[1:], grid) for w in ffn_weights]
    return _pallas_call(
        functools.partial(_mixer_kernel, layer, chunk_decay), f"mixer_l{layer}", grid,
        in_specs=[
            pl.BlockSpec((1, SEQ_TILE, D_MODEL), lambda b, s: (b, s, 0)),
            _layer_spec(layer, (BATCH, N_MOD * D_MODEL)),
            _const_spec((DEPTH, D_MODEL)),
            _layer_spec(layer, (D_MODEL, 2 * RET_QK_WIDTH)),
            _const_spec((D_MODEL, IN_WIDTH)),
            pl.BlockSpec((SEQ_TILE, LANES), lambda b, s: (s, 0)),
            pl.BlockSpec((SEQ_TILE, LANES), lambda b, s: (s, 0)),
            _const_spec((RET_HEADS, RET_CHUNK, RET_CHUNK)),
            _const_spec((RET_HEADS, RET_CHUNK, RET_V_DIM)),
            _const_spec((RET_HEADS, RET_CHUNK, LANES)),
            _const_spec((2 * HG_CHUNK, 2 * HG_CHUNK)),
            _const_spec((2, HG_CHUNK, HG_CHUNK)),
            _const_spec((N_LEVELS * HG_CHUNK, 2 * HG_CHUNK)),
            _const_spec((N_LEVELS + 1, HG_CHUNK, HG_CHUNK)),
            _const_spec((DEPTH, HG_WIDTH)),
            _const_spec((DEPTH, RET_WIDTH)),
            _const_spec((DEPTH, HG_WIDTH)),
            _const_spec((D_MODEL, D_MODEL)),
            *[src for src, _ in casts],
        ],
        operands=(x, mod, norm_w, w_qk, w_in, cos, sin,
                  jnp.asarray(inner), jnp.asarray(cross_b), jnp.asarray(state_b),
                  jnp.asarray(half_tab, dtype=BF16), jnp.asarray(direct_mask),
                  jnp.asarray(level_tab, dtype=BF16), jnp.asarray(level_mask),
                  lbs, ret_w, hg_w, w_out, *ffn_weights),
        out_specs=[pl.BlockSpec((1, SEQ_TILE, D_MODEL), lambda b, s: (b, s, 0)), *[dst for _, dst in casts]],
        out_shapes=[jax.ShapeDtypeStruct((BATCH, SEQ, D_MODEL), F32),
                    *[jax.ShapeDtypeStruct(w.shape[1:], BF16) for w in ffn_weights]],
        scratch=[
            pltpu.VMEM((SEQ_TILE, ACT_WIDTH), F32),
            pltpu.VMEM((n_chunks, 3 * HG_CHUNK, HG_WIDTH), F32),
            pltpu.VMEM((N_LEVELS * HG_CHUNK, HG_WIDTH), F32),
            pltpu.VMEM((n_chunks * HG_HEADS, HG_CHUNK, HG_CHUNK), BF16),
            pltpu.VMEM((SEQ_TILE, D_MODEL), F32),
            pltpu.VMEM((RET_HEADS, LANES, RET_V_DIM), F32),
            pltpu.VMEM((HG_HEADS, HG_DIM, HG_DIM), F32),
        ])


def _ffn_kernel(layer, x_ref, mod_ref, nw_ref, wg_ref, wu_ref, wd_ref, fw_ref, *refs):
    final = layer == DEPTH - 1
    if final:
        o_ref, a_ref = refs
    else:
        win32_ref, wout32_ref, o_ref, win16_ref, wout16_ref, a_ref = refs
        _cast_blocks((win32_ref, wout32_ref), (win16_ref, wout16_ref))
    sh2, sc2, g2 = _mod_vectors(mod_ref, 3)
    for r in range(FFN_TILE // FFN_ROWS):
        rows = slice(r * FFN_ROWS, (r + 1) * FFN_ROWS)
        x = x_ref[0, rows, :]
        h = (_rms(x, nw_ref[layer:layer + 1, :]) * (1.0 + sc2) + sh2).astype(BF16)
        for j in range(D_FF // FFN_CHUNK):
            cs = slice(j * FFN_CHUNK, (j + 1) * FFN_CHUNK)
            a_ref[r, :, cs] = (_silu(_dot(h, wg_ref[:, cs])) * _dot(h, wu_ref[:, cs])).astype(BF16)
        y = x + g2 * _dot(a_ref[r], wd_ref[...])
        if final:
            y = _rms(y, fw_ref[...])
        o_ref[0, rows, :] = y


def _ffn_call(layer, x, mod, norm_w, w_gate, w_up, w_down, final_w, next_mixer_weights):
    final = layer == DEPTH - 1
    grid = (BATCH, SEQ // FFN_TILE)
    to_cast = () if final else next_mixer_weights
    casts = [_cast_specs(layer + 1, *w.shape[1:], grid) for w in to_cast]
    return _pallas_call(
        functools.partial(_ffn_kernel, layer), "swiglu_final" if final else "swiglu", grid,
        in_specs=[
            pl.BlockSpec((1, FFN_TILE, D_MODEL), lambda b, s: (b, s, 0)),
            _layer_spec(layer, (BATCH, N_MOD * D_MODEL)),
            _const_spec((DEPTH, D_MODEL)),
            _const_spec((D_MODEL, D_FF)),
            _const_spec((D_MODEL, D_FF)),
            _const_spec((D_FF, D_MODEL)),
            _const_spec((1, D_MODEL)),
            *[src for src, _ in casts],
        ],
        operands=(x, mod, norm_w, w_gate, w_up, w_down, final_w.reshape(1, D_MODEL), *to_cast),
        out_specs=[pl.BlockSpec((1, FFN_TILE, D_MODEL), lambda b, s: (b, s, 0)), *[dst for _, dst in casts]],
        out_shapes=[jax.ShapeDtypeStruct((BATCH, SEQ, D_MODEL), F32),
                    *[jax.ShapeDtypeStruct(w.shape[1:], BF16) for w in to_cast]],
        scratch=[pltpu.VMEM((FFN_TILE // FFN_ROWS, FFN_ROWS, D_FF), BF16)])


@jax.jit
def kernel(x, c, w_ada, b_ada, norm_mix_w, w_in, ret_norm_w, hg_lower_bounds, hg_norm_w, w_out,
           norm_ffn_w, w_ffn_gate, w_ffn_up, w_ffn_down, final_norm_w):
    assert x.shape == (BATCH, SEQ, D_MODEL) and x.dtype == F32
    mod_all, w_qk, w_in_b, w_out_b = _ada_call(c, w_ada, b_ada, w_in, w_out)
    cos, sin = _rotary_tables()
    ret_tabs = _ret_tables(RET_CHUNK)
    hg_tabs = _hg_tables()
    for layer in range(DEPTH):
        x, w_gate_b, w_up_b, w_down_b = _mixer_call(
            layer, x, mod_all, norm_mix_w, w_qk, w_in_b, cos, sin, ret_tabs, hg_tabs,
            hg_lower_bounds, ret_norm_w, hg_norm_w, w_out_b, (w_ffn_gate, w_ffn_up, w_ffn_down))
        x, *next_weights = _ffn_call(layer, x, mod_all, norm_ffn_w, w_gate_b, w_up_b, w_down_b, final_norm_w,
                                     (w_in, w_out))
        if next_weights:
            w_in_b, w_out_b = next_weights
    return x
```

```python
import functools

import numpy as np
import jax
import jax.numpy as jnp
from jax import lax
from jax.experimental import pallas as pl
from jax.experimental.pallas import tpu as pltpu

D_MODEL = 1024
BATCH = 8
SEQ = 2048
DEPTH = 2
RET_WIDTH = 512
HG_WIDTH = 512
RET_HEADS = 4
RET_V_DIM = 128
RET_QK_DIM = 64
RET_QK_WIDTH = 256
HG_HEADS = 4
HG_DIM = 128
D_FF = 2816
ROPE_BASE = 10000.0
EPS = 1e-6
N_MOD = 6
IN_WIDTH = 3584

OFF_RQ, OFF_RK, OFF_RV, OFF_RG, OFF_HQ, OFF_HF, OFF_HI, OFF_HG = (
    0, 256, 512, 1024, 1536, 2048, 2560, 3072)
OFF_KIN = IN_WIDTH
ACT_WIDTH = IN_WIDTH + HG_WIDTH

LANES = 128
BF16_SUBLANES = 16
PROJ_COLS = 256
SEQ_TILE = 512
RET_CHUNK = 256
HG_CHUNK = 64
FFN_TILE = 1024
FFN_ROWS = 256
FFN_CHUNK = 256
ADA_TILE = 1536
V7X_VMEM_BYTES = 64 * 1024 * 1024
COMPILER_SCRATCH_BYTES = 12 * 1024 * 1024
VMEM_LIMIT_BYTES = V7X_VMEM_BYTES - 8 * 1024 * 1024

F32 = jnp.float32
BF16 = jnp.bfloat16


def _hg_levels():
    out, m = [], HG_CHUNK // 2
    while m >= 1:
        out.append(m)
        m //= 2
    return out


HG_LEVELS = _hg_levels()
N_LEVELS = len(HG_LEVELS)
HG_HALF = HG_CHUNK // 2
HG_DIRECT_LIMIT = 60.0


def _hg_tables():
    c = HG_CHUNK
    r = np.arange(c)
    u = r[None, :]
    half_start = (r // HG_HALF) * HG_HALF
    half_end = half_start + HG_HALF - 1
    m_q = (u >= half_start[:, None]) & (u <= r[:, None])
    m_e = (u > r[:, None]) & (u <= half_end[:, None])
    half_tab = np.concatenate([m_q, m_e], axis=0).astype(np.float32)
    upper = r >= HG_HALF
    across = upper[:, None] & (~upper)[None, :]
    inside = (upper[:, None] == upper[None, :]) & (r[None, :] <= r[:, None])
    direct_mask = np.stack([across, inside]).astype(np.float32)

    mats, masks = [], [np.eye(c, dtype=bool)]
    for m in HG_LEVELS:
        blk, pos = r // (2 * m), r % (2 * m)
        mid = blk * 2 * m + m
        up = pos >= m
        mats.append(np.where(up[:, None], (u >= mid[:, None]) & (u <= r[:, None]),
                             (u > r[:, None]) & (u < mid[:, None])))
        masks.append((blk[:, None] == blk[None, :]) & up[:, None] & (~up)[None, :])
    level_tab = np.concatenate(mats, axis=0).astype(np.float32)
    level_mask = np.stack(masks).astype(np.float32)
    dup = lambda t: np.concatenate([t, t], axis=1)
    return dup(half_tab), direct_mask, dup(level_tab), level_mask


def _ret_tables(chunk):
    h = np.arange(RET_HEADS, dtype=np.float32)
    log_gamma = np.log(np.float32(1.0) - np.power(np.float32(2.0), np.float32(-5.0) - h)).astype(np.float32)
    idx = np.arange(chunk, dtype=np.float32)
    rel = idx[:, None] - idx[None, :]
    inner = np.exp(np.where(rel[None] >= 0, log_gamma[:, None, None] * rel[None], -np.inf)).astype(np.float32)
    cross = np.exp(log_gamma[:, None] * (idx[None, :] + np.float32(1.0))).astype(np.float32)
    state = np.exp(log_gamma[:, None] * (np.float32(chunk) - np.float32(1.0) - idx[None, :])).astype(np.float32)
    chunk_decay = np.exp(log_gamma * np.float32(chunk)).astype(np.float32)
    cross_b = np.broadcast_to(cross[:, :, None], (RET_HEADS, chunk, RET_V_DIM)).copy()
    state_b = np.broadcast_to(state[:, :, None], (RET_HEADS, chunk, LANES)).copy()
    return inner, cross_b, state_b, [float(v) for v in chunk_decay]


RET_HEADS_PER_GROUP = LANES // RET_QK_DIM


def _rotary_tables():
    half = RET_QK_DIM // 2
    inv = np.power(np.float32(ROPE_BASE), -np.linspace(0.0, 1.0, half, dtype=np.float32)).astype(np.float32)
    pos = np.arange(SEQ, dtype=np.float32)
    theta = (pos[:, None] * inv[None, :]).astype(np.float32)
    sin = np.tile(np.sin(theta), (1, RET_HEADS_PER_GROUP))
    cos = np.tile(np.cos(theta), (1, 2 * RET_HEADS_PER_GROUP))
    return cos.astype(np.float32), np.concatenate([-sin, sin], axis=1).astype(np.float32)


def _deinterleave_qk():
    half = RET_QK_DIM // 2
    src = np.arange(2 * RET_QK_WIDTH).reshape(2, RET_HEADS // RET_HEADS_PER_GROUP, RET_HEADS_PER_GROUP, half, 2)
    src = np.moveaxis(src, -1, -3).reshape(-1)
    perm = np.zeros((2 * RET_QK_WIDTH, 2 * RET_QK_WIDTH), np.float32)
    perm[src, np.arange(2 * RET_QK_WIDTH)] = 1.0
    return perm


def _rms(x, w):
    return x * lax.rsqrt(jnp.mean(x * x, axis=-1, keepdims=True) + EPS) * w


def _silu(x):
    return x / (1.0 + jnp.exp(-x))


def _dot(a, b):
    return jnp.dot(a, b, preferred_element_type=F32)


def _dot_nt(a, b):
    return lax.dot_general(a, b, (((1,), (1,)), ((), ())), preferred_element_type=F32)


def _dot_tn(a, b):
    return lax.dot_general(a, b, (((0,), (0,)), ((), ())), preferred_element_type=F32)


def _pallas_call(kernel_fn, name, grid, in_specs, operands, out_specs, out_shapes, scratch=()):
    def block_bytes(spec, dtype):
        n_buffers = 2 if spec.pipeline_mode is None else spec.pipeline_mode.buffer_count
        return n_buffers * int(np.prod([d for d in spec.block_shape if d is not None])) * jnp.dtype(dtype).itemsize

    declared = (sum(block_bytes(s, a.dtype) for s, a in zip(in_specs, operands))
                + sum(block_bytes(s, o.dtype) for s, o in zip(out_specs, out_shapes))
                + sum(int(np.prod(s.shape)) * jnp.dtype(s.dtype).itemsize for s in scratch))
    assert declared + COMPILER_SCRATCH_BYTES <= VMEM_LIMIT_BYTES, (name, declared)
    return pl.pallas_call(
        kernel_fn, grid=grid, in_specs=list(in_specs), out_specs=list(out_specs), out_shape=list(out_shapes),
        scratch_shapes=list(scratch),
        compiler_params=pltpu.CompilerParams(
            dimension_semantics=("arbitrary",) * len(grid), vmem_limit_bytes=VMEM_LIMIT_BYTES),
        name=name,
    )(*operands)


def _ada_kernel(c_ref, w_ref, b_ref, wqk32_ref, perm_ref, win32_ref, wout32_ref,
                o_ref, wqk16_ref, win16_ref, wout16_ref):
    c_act = _silu(c_ref[...])
    bias = b_ref[pl.ds(pl.program_id(0), 1), :]
    o_ref[0] = _dot(c_act.astype(BF16), w_ref[0].astype(BF16)) + bias
    wqk16_ref[...] = _dot(wqk32_ref[...].astype(BF16), perm_ref[...]).astype(BF16)
    _cast_blocks((win32_ref, wout32_ref), (win16_ref, wout16_ref))


def _ada_call(c, w_ada, b_ada, w_in, w_out):
    n = N_MOD * D_MODEL
    grid = (DEPTH, n // ADA_TILE)
    qk_width = 2 * RET_QK_WIDTH
    qk_rows = D_MODEL // grid[1]
    qk_spec = pl.BlockSpec((None, qk_rows, qk_width), lambda l, j: (l, j, 0))
    casts = [_cast_specs(0, *w.shape[1:], grid) for w in (w_in, w_out)]
    return _pallas_call(
        _ada_kernel, "adaln_mod", grid,
        in_specs=[
            pl.BlockSpec((BATCH, D_MODEL), lambda l, j: (0, 0)),
            pl.BlockSpec((1, D_MODEL, ADA_TILE), lambda l, j: (l, 0, j)),
            pl.BlockSpec((DEPTH, ADA_TILE), lambda l, j: (0, j)),
            qk_spec,
            pl.BlockSpec((qk_width, qk_width), lambda l, j: (0, 0)),
            *[src for src, _ in casts],
        ],
        operands=(c, w_ada, b_ada, w_in, jnp.asarray(_deinterleave_qk(), dtype=BF16), w_in, w_out),
        out_specs=[pl.BlockSpec((1, BATCH, ADA_TILE), lambda l, j: (l, 0, j)), qk_spec,
                   *[dst for _, dst in casts]],
        out_shapes=[jax.ShapeDtypeStruct((DEPTH, BATCH, n), F32),
                    jax.ShapeDtypeStruct((DEPTH, D_MODEL, qk_width), BF16),
                    jax.ShapeDtypeStruct(w_in.shape[1:], BF16), jax.ShapeDtypeStruct(w_out.shape[1:], BF16)])


def _mixer_kernel(layer, chunk_decay,
                  x_ref, mod_ref, nw_ref, wqk_ref, win_ref, cos_ref, sin_ref, inner_ref, cross_ref, sdec_ref,
                  htab_ref, dmask_ref, ltab_ref, lmask_ref, lbs_ref, retw_ref, hgw_ref, wout_ref,
                  wg32_ref, wu32_ref, wd32_ref,
                  o_ref, wg16_ref, wu16_ref, wd16_ref,
                  act_ref, dec_ref, lvl_ref, sc_ref, mix_ref, r_ref, st_ref):
    @pl.when(pl.program_id(1) == 0)
    def _():
        r_ref[...] = jnp.zeros_like(r_ref)
        st_ref[...] = jnp.zeros_like(st_ref)

    _cast_blocks((wg32_ref, wu32_ref, wd32_ref), (wg16_ref, wu16_ref, wd16_ref))

    x = x_ref[0]
    sh1, sc1, g1 = _mod_vectors(mod_ref, 0)
    h = (_rms(x, nw_ref[layer:layer + 1, :]) * (1.0 + sc1) + sh1).astype(BF16)

    def proj(off, width):
        w_ref = wqk_ref if off < OFF_RV else win_ref
        return _dot(h, w_ref[:, off:off + width])

    n_chunks = SEQ_TILE // HG_CHUNK
    cos, sin = cos_ref[...], sin_ref[...]
    k_scale = RET_QK_DIM ** -0.5

    if layer > 0:
        lbs = lbs_ref[...]
        ex = jnp.exp(lbs - jnp.max(lbs, axis=0, keepdims=True))
        probs = ex / jnp.sum(ex, axis=0, keepdims=True)
        lower_bound = jnp.sum(probs[1:layer + 1], axis=0, keepdims=True)

    def forget_unit(i):
        cols = slice(i * PROJ_COLS, (i + 1) * PROJ_COLS)
        hf = proj(OFF_HF + cols.start, PROJ_COLS)
        e = jnp.exp(-jnp.abs(hf))
        if layer == 0:
            log_f = jnp.minimum(hf, 0.0) - jnp.log(1.0 + e)
            k_in = jnp.where(hf >= 0.0, e, 1.0) / (1.0 + e)
        else:
            lb = lower_bound[:, cols]
            f = lb + (1.0 - lb) * (jnp.where(hf >= 0.0, 1.0, e) / (1.0 + e))
            log_f = jnp.log(f)
            k_in = 1.0 - f
        act_ref[:, OFF_HF + cols.start:OFF_HF + cols.stop] = log_f
        act_ref[:, OFF_KIN + cols.start:OFF_KIN + cols.stop] = k_in

    def plain_unit(off):
        act_ref[:, off:off + PROJ_COLS] = proj(off, PROJ_COLS)

    def silu_unit(off):
        act_ref[:, off:off + PROJ_COLS] = _silu(proj(off, PROJ_COLS))

    def rotary_unit(off, scale):
        p = proj(off, RET_QK_WIDTH)
        for grp in range(RET_QK_WIDTH // LANES):
            pg = p[:, grp * LANES:(grp + 1) * LANES]
            swapped = pltpu.roll(pg, LANES // 2, 1)
            act_ref[:, off + grp * LANES:off + (grp + 1) * LANES] = (pg * cos + swapped * sin) * scale

    for i in range(HG_WIDTH // PROJ_COLS):
        forget_unit(i)
        plain_unit(OFF_HI + i * PROJ_COLS)
    for i in range(HG_WIDTH // PROJ_COLS):
        silu_unit(OFF_HQ + i * PROJ_COLS)
        plain_unit(OFF_RV + i * PROJ_COLS)
    rotary_unit(OFF_RQ, 1.0)
    rotary_unit(OFF_RK, k_scale)

    def log_f_split(rows):
        g = act_ref[rows, OFF_HF:OFF_HF + HG_WIDTH]
        g_hi = g.astype(BF16)
        g_lo = (g - g_hi.astype(F32)).astype(BF16)
        return jnp.concatenate([g_hi, g_lo], axis=0)

    e_min = jnp.zeros((HG_CHUNK, HG_WIDTH), F32)
    for ci in range(n_chunks):
        expo = _dot(htab_ref[...], log_f_split(slice(ci * HG_CHUNK, (ci + 1) * HG_CHUNK)))
        e_q = expo[:HG_CHUNK]
        e_min = jnp.minimum(e_min, e_q)
        dec_ref[ci, :2 * HG_CHUNK] = jnp.exp(expo)
        dec_ref[ci, 2 * HG_CHUNK:] = jnp.exp(-e_q)
    direct_ok = jnp.min(e_min) >= -HG_DIRECT_LIMIT

    lane = lax.broadcasted_iota(jnp.int32, (1, LANES), 1)
    head_in_group = (lane % (LANES // 2)) // (RET_QK_DIM // 2)
    n_ret = SEQ_TILE // RET_CHUNK

    def group_cols(off, hh):
        start = off + (hh // RET_HEADS_PER_GROUP) * LANES
        return slice(start, start + LANES)

    ret_scores, ret_upd = {}, {}
    for rc in range(n_ret):
        rows = slice(rc * RET_CHUNK, (rc + 1) * RET_CHUNK)
        for hh in range(RET_HEADS):
            q = act_ref[rows, group_cols(OFF_RQ, hh)].astype(BF16)
            kf = act_ref[rows, group_cols(OFF_RK, hh)]
            k_h = jnp.where(head_in_group == hh % RET_HEADS_PER_GROUP, kf, 0.0)
            v_h = act_ref[rows, OFF_RV + hh * RET_V_DIM:OFF_RV + (hh + 1) * RET_V_DIM].astype(BF16)
            ret_scores[rc, hh] = (_dot_nt(q, k_h.astype(BF16)) * inner_ref[hh]).astype(BF16)
            ret_upd[rc, hh] = _dot_tn((k_h * sdec_ref[hh]).astype(BF16), v_h)

    across, inside = dmask_ref[0] > 0.5, dmask_ref[1] > 0.5
    for ci in range(n_chunks):
        rows = slice(ci * HG_CHUNK, (ci + 1) * HG_CHUNK)
        for hh in range(HG_HEADS):
            hs = slice(hh * HG_DIM, (hh + 1) * HG_DIM)
            qf = act_ref[rows, OFF_HQ + hs.start:OFF_HQ + hs.stop]
            kk = act_ref[rows, OFF_KIN + hs.start:OFF_KIN + hs.stop]
            a_q = dec_ref[ci, 0:HG_CHUNK, hs]
            a_e = dec_ref[ci, HG_CHUNK:2 * HG_CHUNK, hs]
            a_qinv = dec_ref[ci, 2 * HG_CHUNK:, hs]
            q_t = (qf * a_q).astype(BF16)
            pair_across = _dot_nt(q_t, (kk * a_e).astype(BF16))
            pair_inside = _dot_nt(q_t, (kk * a_qinv).astype(BF16))
            scores = jnp.where(across, pair_across, jnp.where(inside, pair_inside, 0.0))
            sc_ref[ci * HG_HEADS + hh] = scores.astype(BF16)

    upper_half = lax.broadcasted_iota(jnp.int32, (HG_CHUNK, HG_DIM), 0) >= HG_HALF

    def chunk_decays(ci, hs):
        a_q = dec_ref[ci, 0:HG_CHUNK, hs]
        a_e = dec_ref[ci, HG_CHUNK:2 * HG_CHUNK, hs]
        first_half = a_q[HG_HALF - 1:HG_HALF]
        second_half = a_q[HG_CHUNK - 1:HG_CHUNK]
        d_cum = a_q * jnp.where(upper_half, first_half, 1.0)
        d_end = a_e * jnp.where(upper_half, 1.0, second_half)
        return d_cum, d_end, first_half * second_half

    hg_upd = {}
    for ci in range(n_chunks):
        rows = slice(ci * HG_CHUNK, (ci + 1) * HG_CHUNK)
        for hh in range(HG_HEADS):
            hs = slice(hh * HG_DIM, (hh + 1) * HG_DIM)
            kk = act_ref[rows, OFF_KIN + hs.start:OFF_KIN + hs.stop]
            v = act_ref[rows, OFF_HI + hs.start:OFF_HI + hs.stop].astype(BF16)
            _, d_end, _ = chunk_decays(ci, hs)
            hg_upd[ci, hh] = _dot_tn(v, (kk * d_end).astype(BF16))

    for i in range(RET_WIDTH // PROJ_COLS):
        silu_unit(OFF_RG + i * PROJ_COLS)
    for i in range(HG_WIDTH // PROJ_COLS):
        silu_unit(OFF_HG + i * PROJ_COLS)

    @pl.when(jnp.logical_not(direct_ok))
    def _():
        def chunk_scores(ci, carry):
            rows = pl.ds(pl.multiple_of(ci * HG_CHUNK, HG_CHUNK), HG_CHUNK)
            lvl_ref[...] = jnp.exp(_dot(ltab_ref[...], log_f_split(rows)))
            for hh in range(HG_HEADS):
                hs = slice(hh * HG_DIM, (hh + 1) * HG_DIM)
                qf = act_ref[rows, OFF_HQ + hs.start:OFF_HQ + hs.stop]
                kk = act_ref[rows, OFF_KIN + hs.start:OFF_KIN + hs.stop]
                scores = jnp.where(lmask_ref[0] > 0.5, _dot_nt(qf.astype(BF16), kk.astype(BF16)), 0.0)
                for lv in range(N_LEVELS):
                    d_lv = lvl_ref[lv * HG_CHUNK:(lv + 1) * HG_CHUNK, hs]
                    pair = _dot_nt((qf * d_lv).astype(BF16), (kk * d_lv).astype(BF16))
                    scores = jnp.where(lmask_ref[lv + 1] > 0.5, pair, scores)
                sc_ref[ci * HG_HEADS + hh] = scores.astype(BF16)
            return carry

        lax.fori_loop(0, n_chunks, chunk_scores, 0)

    for hh in range(HG_HEADS):
        hs = slice(hh * HG_DIM, (hh + 1) * HG_DIM)
        st = st_ref[hh]
        for ci in range(n_chunks):
            rows = slice(ci * HG_CHUNK, (ci + 1) * HG_CHUNK)
            qf = act_ref[rows, OFF_HQ + hs.start:OFF_HQ + hs.stop]
            v = act_ref[rows, OFF_HI + hs.start:OFF_HI + hs.stop].astype(BF16)
            d_cum, _, d_all = chunk_decays(ci, hs)
            o = (_dot(sc_ref[ci * HG_HEADS + hh], v)
                 + _dot_nt((qf * d_cum).astype(BF16), st.astype(BF16)))
            st = st * d_all + hg_upd[ci, hh]
            gate = act_ref[rows, OFF_HG + hs.start:OFF_HG + hs.stop]
            mix_ref[rows, RET_WIDTH + hs.start:RET_WIDTH + hs.stop] = _rms(o, hgw_ref[layer:layer + 1, hs]) * gate
        st_ref[hh] = st
    for hh in range(RET_HEADS):
        vs = slice(hh * RET_V_DIM, (hh + 1) * RET_V_DIM)
        r_h = r_ref[hh]
        for rc in range(n_ret):
            rows = slice(rc * RET_CHUNK, (rc + 1) * RET_CHUNK)
            q = act_ref[rows, group_cols(OFF_RQ, hh)].astype(BF16)
            v_h = act_ref[rows, OFF_RV + vs.start:OFF_RV + vs.stop].astype(BF16)
            o = _dot(ret_scores[rc, hh], v_h) + _dot(q, r_h.astype(BF16)) * cross_ref[hh]
            r_h = chunk_decay[hh] * r_h + ret_upd[rc, hh]
            gate = act_ref[rows, OFF_RG + vs.start:OFF_RG + vs.stop]
            mix_ref[rows, vs] = _rms(o, retw_ref[layer:layer + 1, vs]) * gate
        r_ref[hh] = r_h

    mixed = (_dot(mix_ref[:, RET_WIDTH:].astype(BF16), wout_ref[RET_WIDTH:, :])
             + _dot(mix_ref[:, :RET_WIDTH].astype(BF16), wout_ref[:RET_WIDTH, :]))
    o_ref[0] = x + g1 * mixed


def _mod_vectors(mod_ref, first):
    row = mod_ref[pl.ds(pl.program_id(0), 1), :]
    return tuple(row[:, (first + k) * D_MODEL:(first + k + 1) * D_MODEL] for k in range(3))


def _cast_blocks(src_refs, dst_refs):
    for src_ref, dst_ref in zip(src_refs, dst_refs):
        dst_ref[...] = src_ref[...].astype(BF16)


def _const_spec(shape):
    zeros = (0,) * len(shape)
    return pl.BlockSpec(shape, lambda b, s: zeros, pipeline_mode=pl.Buffered(1))


def _layer_spec(layer, shape):
    idx = (layer,) + (0,) * len(shape)
    return pl.BlockSpec((None,) + tuple(shape), lambda b, s: idx, pipeline_mode=pl.Buffered(1))


def _cast_specs(layer, rows_total, cols, grid):
    n_steps = grid[0] * grid[1]
    rows = next(r for r in range(BF16_SUBLANES, rows_total + 1, BF16_SUBLANES)
                if rows_total % r == 0 and rows_total // r <= n_steps)
    last = rows_total // rows - 1

    def block(b, s):
        return jnp.minimum(b * grid[1] + s, last)

    return (pl.BlockSpec((None, rows, cols), lambda b, s: (layer, block(b, s), 0)),
            pl.BlockSpec((rows, cols), lambda b, s: (block(b, s), 0)))


def _mixer_call(layer, x, mod, norm_w, w_qk, w_in, cos, sin, ret_tabs, hg_tabs, lbs, ret_w, hg_w, w_out,
                ffn_weights):
    inner, cross_b, state_b, chunk_decay = ret_tabs
    half_tab, direct_mask, level_tab, level_mask = hg_tabs
    n_chunks = SEQ_TILE // HG_CHUNK
    grid = (BATCH, SEQ // SEQ_TILE)
    casts = [_cast_specs(layer, *w.shape[1:], grid) for w in ffn_weights]
    return _pallas_call(
        functools.partial(_mixer_kernel, layer, chunk_decay), f"mixer_l{layer}", grid,
        in_specs=[
            pl.BlockSpec((1, SEQ_TILE, D_MODEL), lambda b, s: (b, s, 0)),
            _layer_spec(layer, (BATCH, N_MOD * D_MODEL)),
            _const_spec((DEPTH, D_MODEL)),
            _layer_spec(layer, (D_MODEL, 2 * RET_QK_WIDTH)),
            _const_spec((D_MODEL, IN_WIDTH)),
            pl.BlockSpec((SEQ_TILE, LANES), lambda b, s: (s, 0)),
            pl.BlockSpec((SEQ_TILE, LANES), lambda b, s: (s, 0)),
            _const_spec((RET_HEADS, RET_CHUNK, RET_CHUNK)),
            _const_spec((RET_HEADS, RET_CHUNK, RET_V_DIM)),
            _const_spec((RET_HEADS, RET_CHUNK, LANES)),
            _const_spec((2 * HG_CHUNK, 2 * HG_CHUNK)),
            _const_spec((2, HG_CHUNK, HG_CHUNK)),
            _const_spec((N_LEVELS * HG_CHUNK, 2 * HG_CHUNK)),
            _const_spec((N_LEVELS + 1, HG_CHUNK, HG_CHUNK)),
            _const_spec((DEPTH, HG_WIDTH)),
            _const_spec((DEPTH, RET_WIDTH)),
            _const_spec((DEPTH, HG_WIDTH)),
            _const_spec((D_MODEL, D_MODEL)),
            *[src for src, _ in casts],
        ],
        operands=(x, mod, norm_w, w_qk, w_in, cos, sin,
                  jnp.asarray(inner), jnp.asarray(cross_b), jnp.asarray(state_b),
                  jnp.asarray(half_tab, dtype=BF16), jnp.asarray(direct_mask),
                  jnp.asarray(level_tab, dtype=BF16), jnp.asarray(level_mask),
                  lbs, ret_w, hg_w, w_out, *ffn_weights),
        out_specs=[pl.BlockSpec((1, SEQ_TILE, D_MODEL), lambda b, s: (b, s, 0)), *[dst for _, dst in casts]],
        out_shapes=[jax.ShapeDtypeStruct((BATCH, SEQ, D_MODEL), F32),
                    *[jax.ShapeDtypeStruct(w.shape[1:], BF16) for w in ffn_weights]],
        scratch=[
            pltpu.VMEM((SEQ_TILE, ACT_WIDTH), F32),
            pltpu.VMEM((n_chunks, 3 * HG_CHUNK, HG_WIDTH), F32),
            pltpu.VMEM((N_LEVELS * HG_CHUNK, HG_WIDTH), F32),
            pltpu.VMEM((n_chunks * HG_HEADS, HG_CHUNK, HG_CHUNK), BF16),
            pltpu.VMEM((SEQ_TILE, D_MODEL), F32),
            pltpu.VMEM((RET_HEADS, LANES, RET_V_DIM), F32),
            pltpu.VMEM((HG_HEADS, HG_DIM, HG_DIM), F32),
        ])


def _ffn_kernel(layer, x_ref, mod_ref, nw_ref, wg_ref, wu_ref, wd_ref, fw_ref, *refs):
    final = layer == DEPTH - 1
    if final:
        o_ref, a_ref = refs
    else:
        win32_ref, wout32_ref, o_ref, win16_ref, wout16_ref, a_ref = refs
        _cast_blocks((win32_ref, wout32_ref), (win16_ref, wout16_ref))
    sh2, sc2, g2 = _mod_vectors(mod_ref, 3)
    for r in range(FFN_TILE // FFN_ROWS):
        rows = slice(r * FFN_ROWS, (r + 1) * FFN_ROWS)
        x = x_ref[0, rows, :]
        h = (_rms(x, nw_ref[layer:layer + 1, :]) * (1.0 + sc2) + sh2).astype(BF16)
        for j in range(D_FF // FFN_CHUNK):
            cs = slice(j * FFN_CHUNK, (j + 1) * FFN_CHUNK)
            a_ref[r, :, cs] = (_silu(_dot(h, wg_ref[:, cs])) * _dot(h, wu_ref[:, cs])).astype(BF16)
        y = x + g2 * _dot(a_ref[r], wd_ref[...])
        if final:
            y = _rms(y, fw_ref[...])
        o_ref[0, rows, :] = y


def _ffn_call(layer, x, mod, norm_w, w_gate, w_up, w_down, final_w, next_mixer_weights):
    final = layer == DEPTH - 1
    grid = (BATCH, SEQ // FFN_TILE)
    to_cast = () if final else next_mixer_weights
    casts = [_cast_specs(layer + 1, *w.shape[1:], grid) for w in to_cast]
    return _pallas_call(
        functools.partial(_ffn_kernel, layer), "swiglu_final" if final else "swiglu", grid,
        in_specs=[
            pl.BlockSpec((1, FFN_TILE, D_MODEL), lambda b, s: (b, s, 0)),
            _layer_spec(layer, (BATCH, N_MOD * D_MODEL)),
            _const_spec((DEPTH, D_MODEL)),
            _const_spec((D_MODEL, D_FF)),
            _const_spec((D_MODEL, D_FF)),
            _const_spec((D_FF, D_MODEL)),
            _const_spec((1, D_MODEL)),
            *[src for src, _ in casts],
        ],
        operands=(x, mod, norm_w, w_gate, w_up, w_down, final_w.reshape(1, D_MODEL), *to_cast),
        out_specs=[pl.BlockSpec((1, FFN_TILE, D_MODEL), lambda b, s: (b, s, 0)), *[dst for _, dst in casts]],
        out_shapes=[jax.ShapeDtypeStruct((BATCH, SEQ, D_MODEL), F32),
                    *[jax.ShapeDtypeStruct(w.shape[1:], BF16) for w in to_cast]],
        scratch=[pltpu.VMEM((FFN_TILE // FFN_ROWS, FFN_ROWS, D_FF), BF16)])


@jax.jit
def kernel(x, c, w_ada, b_ada, norm_mix_w, w_in, ret_norm_w, hg_lower_bounds, hg_norm_w, w_out,
           norm_ffn_w, w_ffn_gate, w_ffn_up, w_ffn_down, final_norm_w):
    assert x.shape == (BATCH, SEQ, D_MODEL) and x.dtype == F32
    mod_all, w_qk, w_in_b, w_out_b = _ada_call(c, w_ada, b_ada, w_in, w_out)
    cos, sin = _rotary_tables()
    ret_tabs = _ret_tables(RET_CHUNK)
    hg_tabs = _hg_tables()
    for layer in range(DEPTH):
        x, w_gate_b, w_up_b, w_down_b = _mixer_call(
            layer, x, mod_all, norm_mix_w, w_qk, w_in_b, cos, sin, ret_tabs, hg_tabs,
            hg_lower_bounds, ret_norm_w, hg_norm_w, w_out_b, (w_ffn_gate, w_ffn_up, w_ffn_down))
        x, *next_weights = _ffn_call(layer, x, mod_all, norm_ffn_w, w_gate_b, w_up_b, w_down_b, final_norm_w,
                                     (w_in, w_out))
        if next_weights:
            w_in_b, w_out_b = next_weights
    return x
```
